```python
import jax, jax.numpy as jnp
from jax import lax
import numpy as np

D_MODEL = 1024
BATCH = 4
SEQ = 8192
DEPTH = 2

CHUNK = 64
N_MOD = 6
EPS = 1e-6

M_HEADS = 4
M_QK_DIM = 128
M_V_DIM = 256
M_QK_WIDTH = M_HEADS * M_QK_DIM
M_V_WIDTH = M_HEADS * M_V_DIM
CONV_WIDTH = 4

A_HEADS = 8
Q_LORA = 384
KV_LORA = 256
QK_NOPE = 128
QK_ROPE = 64
V_HEAD = 128
A_V_WIDTH = A_HEADS * V_HEAD
ROPE_THETA = 10000.0
Q_BLOCK = 128

N_BRANCHES = 2
IN_SIZES = (M_QK_WIDTH, M_QK_WIDTH, M_V_WIDTH, M_V_WIDTH, M_HEADS, M_HEADS,
            Q_LORA, KV_LORA, QK_ROPE, N_BRANCHES * D_MODEL)
D_IN = sum(IN_SIZES)
IN_SPLIT_POINTS = tuple(sum(IN_SIZES[:i + 1]) for i in range(len(IN_SIZES) - 1))

N_GROUPS = 4
EXPERTS_PER_GROUP = 8
N_EXPERTS = N_GROUPS * EXPERTS_PER_GROUP
TOP_K = 2
D_EXPERT = 256
MOE_BLOCK = 128

kernel_name = "hybrid_mlstm_mla_hmoe_adaln"


def rms_norm(x, g):
    xf = x.astype(jnp.float32)
    y = xf * lax.rsqrt(jnp.mean(xf * xf, axis=-1, keepdims=True) + EPS)
    return (y * g.astype(jnp.float32)).astype(x.dtype)


def modulate(h, shift, scale):
    return h * (1.0 + scale[:, None, :]) + shift[:, None, :]


def apply_rope(x, cos, sin):
    x1, x2 = jnp.split(x, 2, axis=-1)
    return jnp.concatenate([x1 * cos - x2 * sin, x1 * sin + x2 * cos], axis=-1)


def causal_depthwise_conv(x, w, b):
    C = x.shape[-1]
    y = lax.conv_general_dilated(x, w[:, None, :], window_strides=(1,),
                                 padding=((CONV_WIDTH - 1, 0),),
                                 dimension_numbers=('NWC', 'WIO', 'NWC'),
                                 feature_group_count=C)
    return y + b


def mlstm_chunkwise(q, k, v, log_i, log_f):
    B, S, NH, DQK = q.shape
    DV = v.shape[-1]
    nc = S // CHUNK

    def to_chunks(t):
        t = t.reshape(B, nc, CHUNK, NH, *t.shape[3:])
        return jnp.moveaxis(t, (1, 3), (0, 2))

    causal = jnp.tril(jnp.ones((CHUNK, CHUNK), dtype=bool))

    def step(carry, xs):
        C, n, m = carry
        qc, kc, vc, lic, lfc = xs
        b = jnp.cumsum(lfc, axis=-1)
        dmat = b[..., :, None] - b[..., None, :] + lic[..., None, :]
        dmat = jnp.where(causal, dmat, -jnp.inf)
        m_inter = b + m[..., None]
        m_t = jnp.maximum(jnp.max(dmat, axis=-1), m_inter)
        s = jnp.einsum('bhtd,bhsd->bhts', qc, kc) * jnp.exp(dmat - m_t[..., None])
        w_inter = jnp.exp(m_inter - m_t)
        num = (jnp.einsum('bhts,bhsv->bhtv', s, vc)
               + w_inter[..., None] * jnp.einsum('bhtd,bhvd->bhtv', qc, C))
        den = jnp.sum(s, axis=-1) + w_inter * jnp.einsum('bhtd,bhd->bht', qc, n)
        h = num / jnp.maximum(jnp.abs(den), jnp.exp(-m_t))[..., None]
        b_last = b[..., -1]
        a = b_last[..., None] - b + lic
        m_new = jnp.maximum(b_last + m, jnp.max(a, axis=-1))
        decay = jnp.exp(b_last + m - m_new)
        wk = jnp.exp(a - m_new[..., None])
        C_new = decay[..., None, None] * C + jnp.einsum('bhs,bhsv,bhsd->bhvd', wk, vc, kc)
        n_new = decay[..., None] * n + jnp.einsum('bhs,bhsd->bhd', wk, kc)
        return (C_new, n_new, m_new), h

    init = (jnp.zeros((B, NH, DV, DQK), jnp.float32),
            jnp.zeros((B, NH, DQK), jnp.float32),
            jnp.zeros((B, NH), jnp.float32))
    _, h = lax.scan(step, init, (to_chunks(q), to_chunks(k), to_chunks(v),
                                 to_chunks(log_i), to_chunks(log_f)))
    return jnp.moveaxis(h, (0, 2), (1, 3)).reshape(B, S, NH, DV)


def chunk_causal_mla_attention(q_nope, q_rope, k_nope, k_rope, v):
    B, S, H, _ = q_nope.shape
    nqb = S // Q_BLOCK
    key_chunk = jnp.arange(S) // CHUNK
    scale = (QK_NOPE + QK_ROPE) ** -0.5

    def to_blocks(t):
        return t.reshape(B, nqb, Q_BLOCK, *t.shape[2:]).swapaxes(0, 1)

    def one_block(args):
        qn, qr, blk = args
        s = (jnp.einsum('bqhd,bkhd->bhqk', qn, k_nope)
             + jnp.einsum('bqhr,bkr->bhqk', qr, k_rope)).astype(jnp.float32) * scale
        q_chunk = (blk * Q_BLOCK + jnp.arange(Q_BLOCK)) // CHUNK
        mask = key_chunk[None, :] <= q_chunk[:, None]
        s = jnp.where(mask, s, -jnp.inf)
        p = jax.nn.softmax(s, axis=-1).astype(v.dtype)
        return jnp.einsum('bhqk,bkhd->bqhd', p, v)

    out = lax.map(one_block, (to_blocks(q_nope), to_blocks(q_rope), jnp.arange(nqb)))
    return out.swapaxes(0, 1).reshape(B, S, H * V_HEAD)


def mixer_sublayer(h, cos, sin, w_in, conv_w, conv_b, igate_b, fgate_b, q_norm_g,
                   kv_norm_g, w_uq, w_ukv, w_branch_m, w_branch_a, w_out):
    B, S, _ = h.shape
    proj = h @ w_in
    (q_m, k_m, v_m, o_m, i_m, f_m, c_q, c_kv, k_r, gate_pre) = jnp.split(
        proj, IN_SPLIT_POINTS, axis=-1)

    qk = jax.nn.silu(causal_depthwise_conv(jnp.concatenate([q_m, k_m], axis=-1), conv_w, conv_b))
    q_m, k_m = jnp.split(qk.astype(jnp.float32), 2, axis=-1)
    q_m = q_m.reshape(B, S, M_HEADS, M_QK_DIM) * (M_QK_DIM ** -0.5)
    k_m = k_m.reshape(B, S, M_HEADS, M_QK_DIM)
    v_m = v_m.astype(jnp.float32).reshape(B, S, M_HEADS, M_V_DIM)
    log_i = (i_m + igate_b).astype(jnp.float32)
    log_f = jax.nn.log_sigmoid((f_m + fgate_b).astype(jnp.float32))
    y_m = mlstm_chunkwise(q_m, k_m, v_m, log_i, log_f).reshape(B, S, M_V_WIDTH).astype(h.dtype)
    y_m = jax.nn.sigmoid(o_m) * y_m

    q_a = (rms_norm(c_q, q_norm_g) @ w_uq).reshape(B, S, A_HEADS, QK_NOPE + QK_ROPE)
    q_nope = q_a[..., :QK_NOPE]
    q_rope = apply_rope(q_a[..., QK_NOPE:], cos[:, None, :], sin[:, None, :])
    kv = (rms_norm(c_kv, kv_norm_g) @ w_ukv).reshape(B, S, A_HEADS, QK_NOPE + V_HEAD)
    k_nope, v_a = kv[..., :QK_NOPE], kv[..., QK_NOPE:]
    k_rope = apply_rope(k_r, cos, sin)
    y_a = chunk_causal_mla_attention(q_nope, q_rope, k_nope, k_rope, v_a)

    g_m, g_a = jnp.split(jax.nn.sigmoid(gate_pre), 2, axis=-1)
    merged = g_m * (y_m @ w_branch_m) + g_a * (y_a @ w_branch_a)
    return merged @ w_out


def hierarchical_moe(h, w_group, b_group, w_router, b_router, w_gate, w_up, w_down):
    B, S, D = h.shape
    T = B * S
    ht = h.reshape(T, D)
    hf = ht.astype(jnp.float32)
    grp_probs = jax.nn.softmax(hf @ w_group.astype(jnp.float32) + b_group.astype(jnp.float32), axis=-1)
    grp = jnp.argmax(grp_probs, axis=-1)
    p_grp = jnp.take_along_axis(grp_probs, grp[:, None], axis=-1)[:, 0]
    e_logits = (hf @ w_router.astype(jnp.float32) + b_router.astype(jnp.float32)).reshape(
        T, N_GROUPS, EXPERTS_PER_GROUP)
    in_grp = jnp.take_along_axis(e_logits, grp[:, None, None], axis=1)[:, 0]
    top_p, top_e = lax.top_k(jax.nn.softmax(in_grp, axis=-1), TOP_K)
    weights = p_grp[:, None] * top_p / jnp.sum(top_p, axis=-1, keepdims=True)
    expert_id = grp[:, None] * EXPERTS_PER_GROUP + top_e

    n_assign = T * TOP_K
    flat_e = expert_id.reshape(-1).astype(jnp.int32)
    flat_t = jnp.repeat(jnp.arange(T, dtype=jnp.int32), TOP_K)
    flat_w = weights.reshape(-1)
    order = jnp.argsort(flat_e)
    se, st, sw = flat_e[order], flat_t[order], flat_w[order]
    counts = jnp.bincount(flat_e, length=N_EXPERTS)
    padded = (counts + MOE_BLOCK - 1) // MOE_BLOCK * MOE_BLOCK
    starts = jnp.cumsum(counts) - counts
    p_ends = jnp.cumsum(padded)
    p_starts = p_ends - padded
    dest = p_starts[se] + jnp.arange(n_assign, dtype=jnp.int32) - starts[se]
    cap = n_assign + N_EXPERTS * MOE_BLOCK
    n_blocks = cap // MOE_BLOCK
    slot_tok = jnp.zeros((cap,), jnp.int32).at[dest].set(st)
    slot_w = jnp.zeros((cap,), jnp.float32).at[dest].set(sw)
    blk_start = jnp.arange(n_blocks) * MOE_BLOCK
    blk_expert = jnp.minimum(jnp.sum(p_ends[None, :] <= blk_start[:, None], axis=1), N_EXPERTS - 1)
    xg = ht[slot_tok].reshape(n_blocks, MOE_BLOCK, D)

    def run_block(args):
        xb, e, wb = args
        a = jax.nn.silu(xb @ w_gate[e]) * (xb @ w_up[e])
        return (a @ w_down[e]) * wb[:, None].astype(xb.dtype)

    out = lax.map(run_block, (xg, blk_expert, slot_w.reshape(n_blocks, MOE_BLOCK)))
    y = jnp.zeros((T, D), h.dtype).at[slot_tok].add(out.reshape(cap, D))
    return y.reshape(B, S, D)


def setup_inputs(seed: int = 0) -> dict:
    key = jax.random.key(seed)
    ks = jax.random.split(key, 26)
    f32 = jnp.float32

    def nrm(k, shape, scale):
        return jax.random.normal(k, shape, f32) * scale

    def gain(k, shape):
        return 1.0 + 0.02 * jax.random.normal(k, shape, f32)

    L, D = DEPTH, D_MODEL
    return {
        "x": nrm(ks[0], (BATCH, SEQ, D), 1.0),
        "c": nrm(ks[1], (BATCH, D), 1.0),
        "mod_w": nrm(ks[2], (L, D, N_MOD * D), 0.5 * D ** -0.5),
        "mod_b": nrm(ks[3], (L, N_MOD * D), 0.02),
        "norm1_g": gain(ks[4], (L, D)),
        "w_in": nrm(ks[5], (L, D, D_IN), D ** -0.5),
        "conv_w": nrm(ks[6], (L, CONV_WIDTH, 2 * M_QK_WIDTH), CONV_WIDTH ** -0.5),
        "conv_b": nrm(ks[7], (L, 2 * M_QK_WIDTH), 0.02),
        "igate_b": nrm(ks[8], (L, M_HEADS), 0.1),
        "fgate_b": 3.0 + nrm(ks[9], (L, M_HEADS), 0.5),
        "q_norm_g": gain(ks[10], (L, Q_LORA)),
        "kv_norm_g": gain(ks[11], (L, KV_LORA)),
        "w_uq": nrm(ks[12], (L, Q_LORA, A_HEADS * (QK_NOPE + QK_ROPE)), Q_LORA ** -0.5),
        "w_ukv": nrm(ks[13], (L, KV_LORA, A_HEADS * (QK_NOPE + V_HEAD)), KV_LORA ** -0.5),
        "w_branch_m": nrm(ks[14], (L, M_V_WIDTH, D), M_V_WIDTH ** -0.5),
        "w_branch_a": nrm(ks[15], (L, A_V_WIDTH, D), A_V_WIDTH ** -0.5),
        "w_out": nrm(ks[16], (L, D, D), D ** -0.5),
        "norm2_g": gain(ks[17], (L, D)),
        "w_group": nrm(ks[18], (L, D, N_GROUPS), D ** -0.5),
        "b_group": nrm(ks[19], (L, N_GROUPS), 0.01),
        "w_router": nrm(ks[20], (L, D, N_EXPERTS), D ** -0.5),
        "b_router": nrm(ks[21], (L, N_EXPERTS), 0.01),
        "w_expert_gate": nrm(ks[22], (L, N_EXPERTS, D, D_EXPERT), D ** -0.5),
        "w_expert_up": nrm(ks[23], (L, N_EXPERTS, D, D_EXPERT), D ** -0.5),
        "w_expert_down": nrm(ks[24], (L, N_EXPERTS, D_EXPERT, D), D_EXPERT ** -0.5),
        "final_norm_g": gain(ks[25], (D,)),
    }


def reference(x, c, mod_w, mod_b, norm1_g, w_in, conv_w, conv_b, igate_b, fgate_b,
              q_norm_g, kv_norm_g, w_uq, w_ukv, w_branch_m, w_branch_a, w_out, norm2_g,
              w_group, b_group, w_router, b_router, w_expert_gate, w_expert_up,
              w_expert_down, final_norm_g):
    B, S, D = x.shape
    pos = jnp.arange(S, dtype=jnp.float32)
    inv_freq = 1.0 / (ROPE_THETA ** (jnp.arange(0, QK_ROPE, 2, dtype=jnp.float32) / QK_ROPE))
    ang = pos[:, None] * inv_freq[None, :]
    cos = jnp.cos(ang).astype(x.dtype)
    sin = jnp.sin(ang).astype(x.dtype)
    cond = jax.nn.silu(c)
    for l in range(DEPTH):
        mod = cond @ mod_w[l] + mod_b[l]
        shift1, scale1, gate1, shift2, scale2, gate2 = jnp.split(mod, N_MOD, axis=-1)
        h = modulate(rms_norm(x, norm1_g[l]), shift1, scale1)
        x = x + gate1[:, None, :] * mixer_sublayer(
            h, cos, sin, w_in[l], conv_w[l], conv_b[l], igate_b[l], fgate_b[l],
            q_norm_g[l], kv_norm_g[l], w_uq[l], w_ukv[l], w_branch_m[l], w_branch_a[l], w_out[l])
        h = modulate(rms_norm(x, norm2_g[l]), shift2, scale2)
        x = x + gate2[:, None, :] * hierarchical_moe(
            h, w_group[l], b_group[l], w_router[l], b_router[l],
            w_expert_gate[l], w_expert_up[l], w_expert_down[l])
    return rms_norm(x, final_norm_g)
```

```python
import functools

import jax
import jax.numpy as jnp
from jax import lax
from jax.experimental import pallas as pl
from jax.experimental.pallas import tpu as pltpu

F32 = jnp.float32
BF16 = jnp.bfloat16
I32 = jnp.int32

EPS = 1e-6
CHUNK = 64
N_MOD = 6

M_HEADS = 4
M_QK_DIM = 128
M_V_DIM = 256
M_QK_WIDTH = M_HEADS * M_QK_DIM
M_V_WIDTH = M_HEADS * M_V_DIM
CONV_WIDTH = 4

A_HEADS = 8
Q_LORA = 384
KV_LORA = 256
QK_NOPE = 128
QK_ROPE = 64
V_HEAD = 128
A_QK_PAD = 256
ROPE_THETA = 10000.0

N_GROUPS = 4
EXPERTS_PER_GROUP = 8
N_EXPERTS = N_GROUPS * EXPERTS_PER_GROUP
TOP_K = 2
D_EXPERT = 256

LANES = 128
SUBLANES = 8

TM_PROJ = 256
MLSTM_CHUNK = 256
TQ = 512
TK = 512
MOE_ROWS = 256
DISPATCH_ROWS = 2048
TB_COMBINE = 256

VMEM_LIMIT = 56 * 1024 * 1024

_OFF_QK = 0
_OFF_V = _OFF_QK + 2 * M_QK_WIDTH
_OFF_O = _OFF_V + M_V_WIDTH
_OFF_KR = _OFF_O + M_V_WIDTH
_OFF_CQ = _OFF_KR + 2 * QK_ROPE
_OFF_CKV = _OFF_CQ + Q_LORA
_OFF_G = _OFF_CKV + KV_LORA
_N_MAIN = _OFF_G + 2 * 1024


def _dot(a, b):
    return jnp.dot(a, b, preferred_element_type=F32)


def _dot_nt(a, b):
    return lax.dot_general(a, b, (((1,), (1,)), ((), ())), preferred_element_type=F32)


def _dot_tn(a, b):
    return lax.dot_general(a, b, (((0,), (0,)), ((), ())), preferred_element_type=F32)


def _split_bf16(x):
    hi = x.astype(BF16)
    lo = (x - hi.astype(F32)).astype(BF16)
    return hi, lo


def _sigmoid(x):
    return 1.0 / (1.0 + jnp.exp(-x))


def _silu(x):
    return x * _sigmoid(x)


def _rms(x, g):
    return x * lax.rsqrt(jnp.mean(x * x, axis=-1, keepdims=True) + EPS) * g


def _const_spec(shape):
    nd = len(shape)
    return pl.BlockSpec(shape, lambda *_: (0,) * nd, pipeline_mode=pl.Buffered(1))


def _mod_kernel(c_ref, w_ref, b_ref, o_ref):
    c = c_ref[...]
    cond = _silu(c)
    ch, cl = _split_bf16(cond)
    wh, wl = _split_bf16(w_ref[0])
    o_ref[0] = _dot(ch, wh) + _dot(cl, wh) + _dot(ch, wl) + b_ref[0]


def _modulation(c, mod_w, mod_b):
    L, D, N = mod_w.shape
    B = c.shape[0]
    tn = 1536
    return pl.pallas_call(
        _mod_kernel,
        grid=(L, N // tn),
        in_specs=[
            pl.BlockSpec((B, D), lambda l, j: (0, 0)),
            pl.BlockSpec((1, D, tn), lambda l, j: (l, 0, j)),
            pl.BlockSpec((1, 1, tn), lambda l, j: (l, 0, j)),
        ],
        out_specs=pl.BlockSpec((1, B, tn), lambda l, j: (l, 0, j)),
        out_shape=jax.ShapeDtypeStruct((L, B, N), F32),
        compiler_params=pltpu.CompilerParams(
            dimension_semantics=("arbitrary", "arbitrary"), vmem_limit_bytes=VMEM_LIMIT),
        name="adaln_mod",
    )(c, mod_w, mod_b.reshape(L, 1, N))


def _rope128(blk, cs):
    return blk * cs[:, :LANES] + pltpu.roll(blk, QK_ROPE, 1) * cs[:, LANES:]


def _in_proj_kernel(x_ref, mod_ref, g_ref, w_ref, wifc_ref, wift_ref, cw_ref, cb_ref, qng_ref,
                    kvng_ref, wuq_ref, wkn_ref, wv_ref, cs_ref,
                    q_ref, k_ref, v_ref, o_ref, gc_ref, gr_ref, gt_ref, qa_ref, ka_ref, va_ref,
                    tail_sc):
    tm = x_ref.shape[1]

    @pl.when(pl.program_id(1) == 0)
    def _():
        tail_sc[...] = jnp.zeros_like(tail_sc)

    x = x_ref[0]
    shift = mod_ref[0, 0:1, :]
    scale = mod_ref[0, 1:2, :]
    h = _rms(x, g_ref[...]) * (1.0 + scale) + shift
    hb = h.astype(BF16)

    raw = _dot(hb, w_ref[:, _OFF_QK:_OFF_V])
    tail = tail_sc[...]
    row8 = lax.broadcasted_iota(I32, (SUBLANES, 1), 0)
    acc = cb_ref[...] + cw_ref[CONV_WIDTH - 1:CONV_WIDTH, :] * raw
    for j in range(1, CONV_WIDTH):
        sh = pltpu.roll(raw, j, 0)
        fix = pltpu.roll(tail, j, 0)
        first = jnp.where(row8 < j, fix, sh[:SUBLANES])
        sh = jnp.concatenate([first, sh[SUBLANES:]], axis=0)
        acc = acc + cw_ref[CONV_WIDTH - 1 - j:CONV_WIDTH - j, :] * sh
    tail_sc[...] = raw[tm - SUBLANES:, :]
    qk = _silu(acc)
    q_ref[0] = (qk[:, :M_QK_WIDTH] * (M_QK_DIM ** -0.5)).astype(BF16)
    k_ref[0] = qk[:, M_QK_WIDTH:].astype(BF16)

    v_ref[0] = _dot(hb, w_ref[:, _OFF_V:_OFF_O]).astype(BF16)
    o_ref[0] = _dot(hb, w_ref[:, _OFF_O:_OFF_KR]).astype(BF16)
    gt_ref[0] = _dot(hb, w_ref[:, _OFF_G:_N_MAIN]).astype(BF16)
    gc_ref[0] = _dot(hb, wifc_ref[...])
    gr_ref[0] = _dot_nt(wift_ref[...], hb)

    cs = cs_ref[...]
    kr = _rope128(_dot(hb, w_ref[:, _OFF_KR:_OFF_CQ]), cs).astype(BF16)

    cq = _dot(hb, w_ref[:, _OFF_CQ:_OFF_CKV])
    cqn = _rms(cq, qng_ref[...]).astype(BF16)
    qa = _dot(cqn, wuq_ref[...])
    att_scale = (QK_NOPE + QK_ROPE) ** -0.5
    for hh in range(A_HEADS):
        base = hh * A_QK_PAD
        qa_ref[0, hh, :, :QK_NOPE] = (qa[:, base:base + QK_NOPE] * att_scale).astype(BF16)
        roped = _rope128(qa[:, base + QK_NOPE:base + A_QK_PAD], cs)
        qa_ref[0, hh, :, QK_NOPE:] = (roped * att_scale).astype(BF16)

    ckv = _dot(hb, w_ref[:, _OFF_CKV:_OFF_G])
    ckvn = _rms(ckv, kvng_ref[...]).astype(BF16)
    kn = _dot(ckvn, wkn_ref[...]).astype(BF16)
    vv = _dot(ckvn, wv_ref[...]).astype(BF16)
    for hh in range(A_HEADS):
        ka_ref[0, hh, :, :QK_NOPE] = kn[:, hh * QK_NOPE:(hh + 1) * QK_NOPE]
        ka_ref[0, hh, :, QK_NOPE:] = kr
        va_ref[0, hh] = vv[:, hh * V_HEAD:(hh + 1) * V_HEAD]


def _in_proj(x, mod, g, wmain, wifc, wift, conv_w, conv_b, qng, kvng, wuq, wkn, wv, cs):
    B, S, D = x.shape
    tm = TM_PROJ
    ns = S // tm
    tok = lambda c: pl.BlockSpec((1, tm, c), lambda b, s: (b, s, 0))
    head = lambda c: pl.BlockSpec((1, A_HEADS, tm, c), lambda b, s: (b, 0, s, 0))
    out_shape = (
        jax.ShapeDtypeStruct((B, S, M_QK_WIDTH), BF16),
        jax.ShapeDtypeStruct((B, S, M_QK_WIDTH), BF16),
        jax.ShapeDtypeStruct((B, S, M_V_WIDTH), BF16),
        jax.ShapeDtypeStruct((B, S, M_V_WIDTH), BF16),
        jax.ShapeDtypeStruct((B, S, LANES), F32),
        jax.ShapeDtypeStruct((B, SUBLANES, S), F32),
        jax.ShapeDtypeStruct((B, S, 2 * D), BF16),
        jax.ShapeDtypeStruct((B, A_HEADS, S, A_QK_PAD), BF16),
        jax.ShapeDtypeStruct((B, A_HEADS, S, A_QK_PAD), BF16),
        jax.ShapeDtypeStruct((B, A_HEADS, S, V_HEAD), BF16),
    )
    out_specs = (
        tok(M_QK_WIDTH), tok(M_QK_WIDTH), tok(M_V_WIDTH), tok(M_V_WIDTH), tok(LANES),
        pl.BlockSpec((1, SUBLANES, tm), lambda b, s: (b, 0, s)),
        tok(2 * D), head(A_QK_PAD), head(A_QK_PAD), head(V_HEAD),
    )
    in_specs = [
        tok(D),
        pl.BlockSpec((1, N_MOD, D), lambda b, s: (b, 0, 0)),
        _const_spec((1, D)),
        _const_spec(wmain.shape), _const_spec(wifc.shape), _const_spec(wift.shape),
        _const_spec(conv_w.shape), _const_spec(conv_b.shape),
        _const_spec(qng.shape), _const_spec(kvng.shape),
        _const_spec(wuq.shape), _const_spec(wkn.shape), _const_spec(wv.shape),
        pl.BlockSpec((tm, 2 * LANES), lambda b, s: (s, 0)),
    ]
    return pl.pallas_call(
        _in_proj_kernel,
        grid=(B, ns),
        in_specs=in_specs,
        out_specs=out_specs,
        out_shape=out_shape,
        scratch_shapes=[pltpu.VMEM((SUBLANES, 2 * M_QK_WIDTH), F32)],
        compiler_params=pltpu.CompilerParams(
            dimension_semantics=("arbitrary", "arbitrary"), vmem_limit_bytes=VMEM_LIMIT),
        name="in_proj",
    )(x, mod, g, wmain, wifc, wift, conv_w, conv_b, qng, kvng, wuq, wkn, wv, cs)


def _log_sigmoid(x):
    return jnp.minimum(x, 0.0) - jnp.log(1.0 + jnp.exp(-jnp.abs(x)))


def _mlstm_kernel(q_ref, k_ref, v_ref, gc_ref, gr_ref, bc_ref, br_ref, y_ref, c_sc, n_sc, m_sc):
    L = q_ref.shape[1]

    @pl.when(pl.program_id(1) == 0)
    def _():
        c_sc[...] = jnp.zeros_like(c_sc)
        n_sc[...] = jnp.zeros_like(n_sc)
        m_sc[...] = jnp.zeros_like(m_sc)

    rows = lax.broadcasted_iota(I32, (L, L), 0)
    cols = lax.broadcasted_iota(I32, (L, L), 1)
    causal = cols <= rows
    tri = jnp.where(causal, 1.0, 0.0).astype(BF16)
    tri_t = jnp.where(rows <= cols, 1.0, 0.0).astype(BF16)

    gcol = gc_ref[0] + bc_ref[...]
    grow = gr_ref[0] + br_ref[:, :1]
    lf_col = _log_sigmoid(gcol)
    lf_row = _log_sigmoid(grow)
    ch, cl = _split_bf16(lf_col)
    b_col = _dot(tri, ch) + _dot(tri, cl)
    rh, rl = _split_bf16(lf_row)
    b_row = _dot(rh, tri_t) + _dot(rl, tri_t)

    for hh in range(M_HEADS):
        q = q_ref[0, :, hh * M_QK_DIM:(hh + 1) * M_QK_DIM]
        k = k_ref[0, :, hh * M_QK_DIM:(hh + 1) * M_QK_DIM]
        v = v_ref[0, :, hh * M_V_DIM:(hh + 1) * M_V_DIM]
        li_c = gcol[:, hh:hh + 1]
        b_c = b_col[:, M_HEADS + hh:M_HEADS + hh + 1]
        li_r = grow[hh:hh + 1, :]
        b_r = b_row[M_HEADS + hh:M_HEADS + hh + 1, :]
        m_prev = m_sc[hh][:, :1]
        c_prev = c_sc[hh]
        n_prev = n_sc[hh]

        dmat = jnp.where(causal, b_c - b_r + li_r, -jnp.inf)
        m_inter = b_c + m_prev
        m_t = jnp.maximum(jnp.max(dmat, axis=-1, keepdims=True), m_inter)
        s = _dot_nt(q, k) * jnp.exp(dmat - m_t)
        w_inter = jnp.exp(m_inter - m_t)
        num = _dot(s.astype(BF16), v) + w_inter * _dot(q, c_prev.astype(BF16))
        qn = jnp.sum(q.astype(F32) * n_prev, axis=-1, keepdims=True)
        den = jnp.sum(s, axis=-1, keepdims=True) + w_inter * qn
        hout = num / jnp.maximum(jnp.abs(den), jnp.exp(-m_t))
        y_ref[0, :, hh * M_V_DIM:(hh + 1) * M_V_DIM] = hout.astype(y_ref.dtype)

        b_last = b_c[L - 1:L, :]
        a_c = b_last - b_c + li_c
        m_new = jnp.maximum(b_last + m_prev, jnp.max(a_c, axis=0, keepdims=True))
        decay = jnp.exp(b_last + m_prev - m_new)
        wk = jnp.exp(a_c - m_new)
        wv = (wk * v.astype(F32)).astype(BF16)
        c_sc[hh] = decay * c_prev + _dot_tn(k, wv)
        n_sc[hh] = decay * n_prev + jnp.sum(wk * k.astype(F32), axis=0, keepdims=True)
        m_sc[hh] = jnp.broadcast_to(m_new, (1, LANES))


def _mlstm(q, k, v, gc, gr, bias_c, bias_r):
    B, S, _ = q.shape
    L = MLSTM_CHUNK
    tok = lambda c: pl.BlockSpec((1, L, c), lambda b, s: (b, s, 0))
    return pl.pallas_call(
        _mlstm_kernel,
        grid=(B, S // L),
        in_specs=[
            tok(M_QK_WIDTH), tok(M_QK_WIDTH), tok(M_V_WIDTH), tok(LANES),
            pl.BlockSpec((1, SUBLANES, L), lambda b, s: (b, 0, s)),
            _const_spec(bias_c.shape), _const_spec(bias_r.shape),
        ],
        out_specs=tok(M_V_WIDTH),
        out_shape=jax.ShapeDtypeStruct((B, S, M_V_WIDTH), BF16),
        scratch_shapes=[
            pltpu.VMEM((M_HEADS, M_QK_DIM, M_V_DIM), F32),
            pltpu.VMEM((M_HEADS, 1, M_QK_DIM), F32),
            pltpu.VMEM((M_HEADS, 1, LANES), F32),
        ],
        compiler_params=pltpu.CompilerParams(
            dimension_semantics=("arbitrary", "arbitrary"), vmem_limit_bytes=VMEM_LIMIT),
        name="mlstm",
    )(q, k, v, gc, gr, bias_c, bias_r)


def _attn_kernel(q_ref, k_ref, v_ref, o_ref, m_sc, l_sc, acc_sc):
    tq = q_ref.shape[2]
    qi = pl.program_id(2)
    q = q_ref[0, 0]

    m_sc[...] = jnp.full_like(m_sc, -jnp.inf)
    l_sc[...] = jnp.zeros_like(l_sc)
    acc_sc[...] = jnp.zeros_like(acc_sc)

    def update(start, mask):
        k = k_ref[0, 0, pl.ds(start, TK), :]
        v = v_ref[0, 0, pl.ds(start, TK), :]
        s = _dot_nt(q, k)
        if mask is not None:
            s = jnp.where(mask, s, -jnp.inf)
        m_prev = m_sc[...]
        m_cur = jnp.maximum(m_prev, jnp.max(s, axis=-1, keepdims=True))
        alpha = jnp.exp(m_prev - m_cur)
        p = jnp.exp(s - m_cur)
        l_sc[...] = alpha * l_sc[...] + jnp.sum(p, axis=-1, keepdims=True)
        acc_sc[...] = alpha * acc_sc[...] + _dot(p.astype(BF16), v)
        m_sc[...] = m_cur

    def body(j, carry):
        update(pl.multiple_of(j * TK, TK), None)
        return carry

    lax.fori_loop(0, qi * (tq // TK), body, 0)

    for jj in range(tq // TK):
        qc = lax.broadcasted_iota(I32, (tq, TK), 0) // CHUNK
        kc = (lax.broadcasted_iota(I32, (tq, TK), 1) + jj * TK) // CHUNK
        update(pl.multiple_of(qi * tq + jj * TK, TK), kc <= qc)

    o_ref[0] = (acc_sc[...] / l_sc[...]).astype(o_ref.dtype)


def _attention(qa, ka, va):
    B, H, S, _ = qa.shape
    return pl.pallas_call(
        _attn_kernel,
        grid=(B, H, S // TQ),
        in_specs=[
            pl.BlockSpec((1, 1, TQ, A_QK_PAD), lambda b, h, i: (b, h, i, 0)),
            pl.BlockSpec((1, 1, S, A_QK_PAD), lambda b, h, i: (b, h, 0, 0)),
            pl.BlockSpec((1, 1, S, V_HEAD), lambda b, h, i: (b, h, 0, 0)),
        ],
        out_specs=pl.BlockSpec((1, TQ, V_HEAD), lambda b, h, i: (b, i, h)),
        out_shape=jax.ShapeDtypeStruct((B, S, H * V_HEAD), BF16),
        scratch_shapes=[
            pltpu.VMEM((TQ, 1), F32), pltpu.VMEM((TQ, 1), F32), pltpu.VMEM((TQ, V_HEAD), F32),
        ],
        compiler_params=pltpu.CompilerParams(
            dimension_semantics=("arbitrary", "arbitrary", "arbitrary"),
            vmem_limit_bytes=VMEM_LIMIT),
        name="mla_attention",
    )(qa, ka, va)


_ROUTE_LANE0 = N_GROUPS


def _merge_kernel(x_ref, ym_ref, o_ref, ya_ref, gt_ref, mod_ref, g2_ref, wm_ref, wa_ref, wo_ref,
                  wrh_ref, wrl_ref, br_ref,
                  xo_ref, h2_ref, ri_ref, rw_ref, cnt_ref, carry_sc):
    tm = x_ref.shape[1]
    D = x_ref.shape[2]

    @pl.when((pl.program_id(0) == 0) & (pl.program_id(1) == 0))
    def _():
        carry_sc[...] = jnp.zeros_like(carry_sc)

    ym = ym_ref[0].astype(F32) * _sigmoid(o_ref[0].astype(F32))
    bm = _dot(ym.astype(BF16), wm_ref[...])
    ba = _dot(ya_ref[0], wa_ref[...])
    gt = gt_ref[0].astype(F32)
    merged = _sigmoid(gt[:, :D]) * bm + _sigmoid(gt[:, D:]) * ba
    out = _dot(merged.astype(BF16), wo_ref[...])
    xn = x_ref[0] + mod_ref[0, 2:3, :] * out
    xo_ref[0] = xn

    h2 = _rms(xn, g2_ref[...]) * (1.0 + mod_ref[0, 4:5, :]) + mod_ref[0, 3:4, :]
    h2_ref[0] = h2

    hh, hl = _split_bf16(h2)
    logits = _dot(hh, wrh_ref[...]) + _dot(hl, wrh_ref[...]) + _dot(hh, wrl_ref[...]) + br_ref[...]
    lane = lax.broadcasted_iota(I32, (tm, LANES), 1).astype(F32)
    big = float(LANES)
    is_grp = lane < N_GROUPS
    gl = jnp.where(is_grp, logits, -jnp.inf)
    gmax = jnp.max(gl, axis=-1, keepdims=True)
    grp = jnp.min(jnp.where(gl == gmax, lane, big), axis=-1, keepdims=True)
    p_grp = 1.0 / jnp.sum(jnp.where(is_grp, jnp.exp(logits - gmax), 0.0), axis=-1, keepdims=True)
    lo = _ROUTE_LANE0 + grp * EXPERTS_PER_GROUP
    el = jnp.where((lane >= lo) & (lane < lo + EXPERTS_PER_GROUP), logits, -jnp.inf)
    m1 = jnp.max(el, axis=-1, keepdims=True)
    i1 = jnp.min(jnp.where(el == m1, lane, big), axis=-1, keepdims=True)
    el2 = jnp.where(lane == i1, -jnp.inf, el)
    m2 = jnp.max(el2, axis=-1, keepdims=True)
    i2 = jnp.min(jnp.where(el2 == m2, lane, big), axis=-1, keepdims=True)
    r = jnp.exp(m2 - m1)
    w1 = p_grp / (1.0 + r)
    w2 = p_grp * r / (1.0 + r)

    hot1 = lane == i1
    hot2 = lane == i2
    onehot = jnp.where(hot1 | hot2, 1.0, 0.0)
    rows = lax.broadcasted_iota(I32, (tm, tm), 0)
    cols = lax.broadcasted_iota(I32, (tm, tm), 1)
    strict = jnp.where(cols < rows, 1.0, 0.0).astype(BF16)
    before = _dot(strict, onehot.astype(BF16)) + carry_sc[...]
    r1 = jnp.sum(jnp.where(hot1, before, 0.0), axis=-1, keepdims=True)
    r2 = jnp.sum(jnp.where(hot2, before, 0.0), axis=-1, keepdims=True)
    carry = carry_sc[...] + jnp.sum(onehot, axis=0, keepdims=True)
    carry_sc[...] = carry
    cnt_ref[...] = carry

    route = jnp.where(lane == 0, i1 - _ROUTE_LANE0,
                      jnp.where(lane == 1, i2 - _ROUTE_LANE0,
                                jnp.where(lane == 2, r1, jnp.where(lane == 3, r2, 0.0))))
    ri_ref[0] = route.astype(I32)
    rw_ref[0] = jnp.where(lane == 0, w1, jnp.where(lane == 1, w2, 0.0))


def _merge(x, ym, o, ya, gt, mod, g2, wm, wa, wo, wrh, wrl, br):
    B, S, D = x.shape
    tm = TM_PROJ
    tok = lambda c: pl.BlockSpec((1, tm, c), lambda b, s: (b, s, 0))
    out_shape = (
        jax.ShapeDtypeStruct((B, S, D), F32),
        jax.ShapeDtypeStruct((B, S, D), F32),
        jax.ShapeDtypeStruct((B, S, LANES), I32),
        jax.ShapeDtypeStruct((B, S, LANES), F32),
        jax.ShapeDtypeStruct((1, LANES), F32),
    )
    out_specs = (tok(D), tok(D), tok(LANES), tok(LANES),
                 pl.BlockSpec((1, LANES), lambda b, s: (0, 0)))
    in_specs = [
        tok(D), tok(M_V_WIDTH), tok(M_V_WIDTH), tok(A_HEADS * V_HEAD), tok(2 * D),
        pl.BlockSpec((1, N_MOD, D), lambda b, s: (b, 0, 0)),
        _const_spec((1, D)),
        _const_spec(wm.shape), _const_spec(wa.shape), _const_spec(wo.shape),
        _const_spec(wrh.shape), _const_spec(wrl.shape), _const_spec(br.shape),
    ]
    return pl.pallas_call(
        _merge_kernel,
        grid=(B, S // tm),
        in_specs=in_specs,
        out_specs=out_specs,
        out_shape=out_shape,
        scratch_shapes=[pltpu.VMEM((1, LANES), F32)],
        compiler_params=pltpu.CompilerParams(
            dimension_semantics=("arbitrary", "arbitrary"), vmem_limit_bytes=VMEM_LIMIT),
        name="merge_router",
    )(x, ym, o, ya, gt, mod, g2, wm, wa, wo, wrh, wrl, br)


def _row_copy(src, si, dst, di, sem):
    return pltpu.make_async_copy(src.at[pl.ds(si, 1), :], dst.at[pl.ds(di, 1), :], sem)


def _dispatch_kernel(pstart_ref, padlo_ref, pend_ref, eid_ref, rank_ref, h_ref, xg_ref,
                     zero_sc, sem, zsem):
    n = eid_ref.shape[0]
    cap = xg_ref.shape[0]
    step = pl.program_id(0)
    base = step * n

    @pl.when(step == 0)
    def _():
        zero_sc[...] = jnp.zeros_like(zero_sc)

        def fill(lo, hi):
            def one(r, c):
                _row_copy(zero_sc, 0, xg_ref, r, zsem).start()
                return c
            lax.fori_loop(lo, hi, one, 0)

        def per_expert(e, c):
            fill(padlo_ref[e], pend_ref[e])
            return c
        lax.fori_loop(0, N_EXPERTS, per_expert, 0)
        fill(pend_ref[N_EXPERTS - 1], cap)

    def issue(a, c):
        dst = pstart_ref[eid_ref[a]] + rank_ref[a]
        _row_copy(h_ref, (base + a) // TOP_K, xg_ref, dst, sem).start()
        return c
    lax.fori_loop(0, n, issue, 0)

    def drain(a, c):
        _row_copy(h_ref, 0, xg_ref, 0, sem).wait()
        return c
    lax.fori_loop(0, n, drain, 0)

    @pl.when(step == pl.num_programs(0) - 1)
    def _():
        n_fill = cap - pl.num_programs(0) * n

        def zdrain(a, c):
            _row_copy(zero_sc, 0, xg_ref, 0, zsem).wait()
            return c
        lax.fori_loop(0, n_fill, zdrain, 0)


def _dispatch(pstart, padlo, pend, eid, rank, h2, cap):
    T, D = h2.shape
    n = DISPATCH_ROWS
    smem_blk = pl.BlockSpec((n,), lambda i, *_: (i,), memory_space=pltpu.SMEM)
    grid_spec = pltpu.PrefetchScalarGridSpec(
        num_scalar_prefetch=3,
        grid=(T * TOP_K // n,),
        in_specs=[smem_blk, smem_blk, pl.BlockSpec(memory_space=pl.ANY)],
        out_specs=pl.BlockSpec(memory_space=pl.ANY),
        scratch_shapes=[pltpu.VMEM((SUBLANES, D), F32), pltpu.SemaphoreType.DMA,
                        pltpu.SemaphoreType.DMA],
    )
    return pl.pallas_call(
        _dispatch_kernel,
        grid_spec=grid_spec,
        out_shape=jax.ShapeDtypeStruct((cap, D), F32),
        compiler_params=pltpu.CompilerParams(dimension_semantics=("arbitrary",)),
        name="moe_dispatch",
    )(pstart, padlo, pend, eid, rank, h2)


def _expert_kernel(blk_e_ref, nused_ref, x_ref, wgu_ref, wd_ref, o_ref):
    i = pl.program_id(0)

    @pl.when(i < nused_ref[0])
    def _():
        xb = x_ref[...].astype(BF16)
        gu = _dot(xb, wgu_ref[0])
        a = _silu(gu[:, :D_EXPERT]) * gu[:, D_EXPERT:]
        o_ref[...] = _dot(a.astype(BF16), wd_ref[0])

    @pl.when(i >= nused_ref[0])
    def _():
        o_ref[...] = jnp.zeros_like(o_ref)


def _experts(blk_e, nused, xg, wgu, wd):
    cap, D = xg.shape
    R = MOE_ROWS
    grid_spec = pltpu.PrefetchScalarGridSpec(
        num_scalar_prefetch=2,
        grid=(cap // R,),
        in_specs=[
            pl.BlockSpec((R, D), lambda i, be, nu: (i, 0)),
            pl.BlockSpec((1, D, 2 * D_EXPERT), lambda i, be, nu: (be[i], 0, 0)),
            pl.BlockSpec((1, D_EXPERT, D), lambda i, be, nu: (be[i], 0, 0)),
        ],
        out_specs=pl.BlockSpec((R, D), lambda i, be, nu: (i, 0)),
    )
    return pl.pallas_call(
        _expert_kernel,
        grid_spec=grid_spec,
        out_shape=jax.ShapeDtypeStruct((cap, D), F32),
        compiler_params=pltpu.CompilerParams(
            dimension_semantics=("arbitrary",), vmem_limit_bytes=VMEM_LIMIT),
        name="moe_experts",
    )(blk_e, nused, xg, wgu, wd)


def _combine_kernel(pstart_ref, eid_ref, rank_ref, og_ref, w_ref, x_ref, mod_ref, fg_ref, o_ref,
                    buf_sc, sem, *, final_norm):
    tb = x_ref.shape[1]

    def issue(t, c):
        for kk in range(TOP_K):
            a = t * TOP_K + kk
            src = pstart_ref[eid_ref[a]] + rank_ref[a]
            pltpu.make_async_copy(og_ref.at[pl.ds(src, 1), :],
                                  buf_sc.at[kk, pl.ds(t, 1), :], sem).start()
        return c
    lax.fori_loop(0, tb, issue, 0)

    def drain(t, c):
        for kk in range(TOP_K):
            pltpu.make_async_copy(og_ref.at[pl.ds(0, 1), :],
                                  buf_sc.at[kk, pl.ds(0, 1), :], sem).wait()
        return c
    lax.fori_loop(0, tb, drain, 0)

    w = w_ref[0]
    y = w[:, 0:1] * buf_sc[0] + w[:, 1:2] * buf_sc[1]
    xo = x_ref[0] + mod_ref[0, 5:6, :] * y
    if final_norm:
        xo = _rms(xo, fg_ref[...])
    o_ref[0] = xo


def _combine(pstart, eid, rank, outg, rw, x, mod, fg, final_norm):
    B, S, D = x.shape
    tb = TB_COMBINE
    ns = S // tb
    grid_spec = pltpu.PrefetchScalarGridSpec(
        num_scalar_prefetch=1,
        grid=(B, ns),
        in_specs=[
            pl.BlockSpec((tb * TOP_K,), lambda b, s, *_: (b * ns + s,), memory_space=pltpu.SMEM),
            pl.BlockSpec((tb * TOP_K,), lambda b, s, *_: (b * ns + s,), memory_space=pltpu.SMEM),
            pl.BlockSpec(memory_space=pl.ANY),
            pl.BlockSpec((1, tb, LANES), lambda b, s, *_: (b, s, 0)),
            pl.BlockSpec((1, tb, D), lambda b, s, *_: (b, s, 0)),
            pl.BlockSpec((1, N_MOD, D), lambda b, s, *_: (b, 0, 0)),
            pl.BlockSpec((1, D), lambda b, s, *_: (0, 0)),
        ],
        out_specs=pl.BlockSpec((1, tb, D), lambda b, s, *_: (b, s, 0)),
        scratch_shapes=[pltpu.VMEM((TOP_K, tb, D), F32), pltpu.SemaphoreType.DMA],
    )
    return pl.pallas_call(
        functools.partial(_combine_kernel, final_norm=final_norm),
        grid_spec=grid_spec,
        out_shape=jax.ShapeDtypeStruct((B, S, D), F32),
        compiler_params=pltpu.CompilerParams(
            dimension_semantics=("arbitrary", "arbitrary"), vmem_limit_bytes=VMEM_LIMIT),
        name="moe_combine",
    )(pstart, eid, rank, outg, rw, x, mod, fg)


def _rotate_half_cols(w):
    half = w.shape[-1] // 2
    return jnp.concatenate([-w[..., half:], w[..., :half]], axis=-1)


def _pack_layer_weights(l, w_in, w_uq, w_ukv, w_group, w_router, b_group, b_router,
                        w_expert_gate, w_expert_up, igate_b, fgate_b):
    D = w_in.shape[1]
    sizes = (M_QK_WIDTH, M_QK_WIDTH, M_V_WIDTH, M_V_WIDTH, M_HEADS, M_HEADS, Q_LORA, KV_LORA,
             QK_ROPE, 2 * D)
    pts = [sum(sizes[:i + 1]) for i in range(len(sizes) - 1)]
    wq, wk, wv, wo, wi, wf, wcq, wckv, wkr, wg = jnp.split(w_in[l], pts, axis=1)
    wmain = jnp.concatenate([wq, wk, wv, wo, wkr, _rotate_half_cols(wkr), wcq, wckv, wg],
                            axis=1).astype(BF16)
    wif = jnp.concatenate([wi, wf], axis=1)
    wifc = jnp.pad(wif, ((0, 0), (0, LANES - 2 * M_HEADS))).astype(BF16)
    wift = wif.T.astype(BF16)

    uq = w_uq[l].reshape(Q_LORA, A_HEADS, QK_NOPE + QK_ROPE)
    uq_rope = uq[:, :, QK_NOPE:]
    wuq = jnp.concatenate([uq[:, :, :QK_NOPE], uq_rope, _rotate_half_cols(uq_rope)], axis=-1)
    wuq = wuq.reshape(Q_LORA, A_HEADS * A_QK_PAD).astype(BF16)
    ukv = w_ukv[l].reshape(KV_LORA, A_HEADS, QK_NOPE + V_HEAD)
    wkn = ukv[:, :, :QK_NOPE].reshape(KV_LORA, A_HEADS * QK_NOPE).astype(BF16)
    wvv = ukv[:, :, QK_NOPE:].reshape(KV_LORA, A_HEADS * V_HEAD).astype(BF16)

    wr = jnp.concatenate([w_group[l], w_router[l]], axis=1)
    wr = jnp.pad(wr, ((0, 0), (0, LANES - wr.shape[1])))
    wrh = wr.astype(BF16)
    wrl = (wr - wrh.astype(F32)).astype(BF16)
    br = jnp.concatenate([b_group[l], b_router[l]])
    br = jnp.pad(br, (0, LANES - br.shape[0])).reshape(1, LANES)

    wgu = jnp.concatenate([w_expert_gate[l], w_expert_up[l]], axis=-1).astype(BF16)

    gb = jnp.concatenate([igate_b[l], fgate_b[l]])
    bias_c = jnp.pad(gb, (0, LANES - gb.shape[0])).reshape(1, LANES)
    bias_r = jnp.broadcast_to(gb[:, None], (2 * M_HEADS, LANES))
    return wmain, wifc, wift, wuq, wkn, wvv, wrh, wrl, br, wgu, bias_c, bias_r


def kernel(x, c, mod_w, mod_b, norm1_g, w_in, conv_w, conv_b, igate_b, fgate_b, q_norm_g,
           kv_norm_g, w_uq, w_ukv, w_branch_m, w_branch_a, w_out, norm2_g, w_group, b_group,
           w_router, b_router, w_expert_gate, w_expert_up, w_expert_down, final_norm_g):
    B, S, D = x.shape
    L = mod_w.shape[0]
    T = B * S
    assert S % TQ == 0 and S % MLSTM_CHUNK == 0 and S % TM_PROJ == 0 and S % TB_COMBINE == 0
    assert TQ % TK == 0 and TQ % CHUNK == 0 and (T * TOP_K) % DISPATCH_ROWS == 0

    pos = jnp.arange(S, dtype=F32)
    inv_freq = 1.0 / (ROPE_THETA ** (jnp.arange(0, QK_ROPE, 2, dtype=F32) / QK_ROPE))
    ang = pos[:, None] * inv_freq[None, :]
    cos = jnp.cos(ang)
    sin = jnp.sin(ang)
    z = jnp.zeros((S, QK_ROPE), F32)
    cs = jnp.concatenate([cos, cos, z, sin, sin, z], axis=1)

    mod_all = _modulation(c, mod_w, mod_b).reshape(L, B, N_MOD, D)

    cap = T * TOP_K + N_EXPERTS * MOE_ROWS
    n_blocks = cap // MOE_ROWS

    for l in range(L):
        (wmain, wifc, wift, wuq, wkn, wvv, wrh, wrl, br, wgu, bias_c, bias_r) = _pack_layer_weights(
            l, w_in, w_uq, w_ukv, w_group, w_router, b_group, b_router, w_expert_gate,
            w_expert_up, igate_b, fgate_b)
        mod = mod_all[l]

        q, k, v, o, gc, gr, gt, qa, ka, va = _in_proj(
            x, mod, norm1_g[l].reshape(1, D), wmain, wifc, wift, conv_w[l],
            conv_b[l].reshape(1, -1), q_norm_g[l].reshape(1, -1), kv_norm_g[l].reshape(1, -1),
            wuq, wkn, wvv, cs)
        ym = _mlstm(q, k, v, gc, gr, bias_c, bias_r)
        ya = _attention(qa, ka, va)
        xn, h2, ri, rw, cnt = _merge(
            x, ym, o, ya, gt, mod, norm2_g[l].reshape(1, D), w_branch_m[l].astype(BF16),
            w_branch_a[l].astype(BF16), w_out[l].astype(BF16), wrh, wrl, br)

        counts = cnt[0, _ROUTE_LANE0:_ROUTE_LANE0 + N_EXPERTS].astype(I32)
        padded = (counts + MOE_ROWS - 1) // MOE_ROWS * MOE_ROWS
        pend = jnp.cumsum(padded).astype(I32)
        pstart = pend - padded
        padlo = pstart + counts
        blk_start = jnp.arange(n_blocks, dtype=I32) * MOE_ROWS
        blk_e = jnp.minimum(jnp.sum(pend[None, :] <= blk_start[:, None], axis=1),
                            N_EXPERTS - 1).astype(I32)
        nused = (pend[N_EXPERTS - 1:] // MOE_ROWS).astype(I32)
        eid = ri[:, :, 0:TOP_K].reshape(T * TOP_K)
        rank = ri[:, :, TOP_K:2 * TOP_K].reshape(T * TOP_K)

        xg = _dispatch(pstart, padlo, pend, eid, rank, h2.reshape(T, D), cap)
        outg = _experts(blk_e, nused, xg, wgu, w_expert_down[l].astype(BF16))
        x = _combine(pstart, eid, rank, outg, rw, xn, mod, final_norm_g.reshape(1, D),
                     final_norm=(l == L - 1))
    return x
```

```python
import functools
import math

import jax
import jax.numpy as jnp
from jax import lax
from jax.experimental import pallas as pl
from jax.experimental.pallas import tpu as pltpu

F32 = jnp.float32
BF16 = jnp.bfloat16
I32 = jnp.int32

EPS = 1e-6
CHUNK = 64
N_MOD = 6

M_HEADS = 4
M_QK_DIM = 128
M_V_DIM = 256
M_QK_WIDTH = M_HEADS * M_QK_DIM
M_V_WIDTH = M_HEADS * M_V_DIM
CONV_WIDTH = 4

A_HEADS = 8
Q_LORA = 384
KV_LORA = 256
QK_NOPE = 128
QK_ROPE = 64
V_HEAD = 128
A_QK_PAD = 256
ROPE_THETA = 10000.0

N_GROUPS = 4
EXPERTS_PER_GROUP = 8
N_EXPERTS = N_GROUPS * EXPERTS_PER_GROUP
TOP_K = 2
D_EXPERT = 256

LANES = 128
SUBLANES = 8

TM_PROJ = 256
MLSTM_CHUNK = 256
TQ = 512
TK = 512
ATTN_HEADS_PER_STEP = 4
MOE_ROWS = 256
DISPATCH_ROWS = 2048
TB_COMBINE = 256

VMEM_LIMIT = 56 * 1024 * 1024

_OFF_QK = 0
_OFF_V = _OFF_QK + 2 * M_QK_WIDTH
_OFF_O = _OFF_V + M_V_WIDTH
_OFF_KR = _OFF_O + M_V_WIDTH
_OFF_CQ = _OFF_KR + 2 * QK_ROPE
_OFF_CKV = _OFF_CQ + Q_LORA
_OFF_G = _OFF_CKV + KV_LORA
_N_MAIN = _OFF_G + 2 * 1024


def _dot(a, b):
    return jnp.dot(a, b, preferred_element_type=F32)


def _dot_nt(a, b):
    return lax.dot_general(a, b, (((1,), (1,)), ((), ())), preferred_element_type=F32)


def _dot_tn(a, b):
    return lax.dot_general(a, b, (((0,), (0,)), ((), ())), preferred_element_type=F32)


def _split_bf16(x):
    hi = x.astype(BF16)
    lo = (x - hi.astype(F32)).astype(BF16)
    return hi, lo


def _sigmoid(x):
    return 1.0 / (1.0 + jnp.exp(-x))


def _silu(x):
    return x * _sigmoid(x)


def _rms(x, g):
    return x * lax.rsqrt(jnp.mean(x * x, axis=-1, keepdims=True) + EPS) * g


def _const_spec(shape):
    nd = len(shape)
    return pl.BlockSpec(shape, lambda *_: (0,) * nd, pipeline_mode=pl.Buffered(1))


def _mod_kernel(c_ref, w_ref, b_ref, o_ref):
    c = c_ref[...]
    cond = _silu(c)
    ch, cl = _split_bf16(cond)
    wh, wl = _split_bf16(w_ref[0])
    o_ref[0] = _dot(ch, wh) + _dot(cl, wh) + _dot(ch, wl) + b_ref[0]


def _modulation(c, mod_w, mod_b):
    L, D, N = mod_w.shape
    B = c.shape[0]
    tn = 1536
    return pl.pallas_call(
        _mod_kernel,
        grid=(L, N // tn),
        in_specs=[
            pl.BlockSpec((B, D), lambda l, j: (0, 0)),
            pl.BlockSpec((1, D, tn), lambda l, j: (l, 0, j)),
            pl.BlockSpec((1, 1, tn), lambda l, j: (l, 0, j)),
        ],
        out_specs=pl.BlockSpec((1, B, tn), lambda l, j: (l, 0, j)),
        out_shape=jax.ShapeDtypeStruct((L, B, N), F32),
        compiler_params=pltpu.CompilerParams(
            dimension_semantics=("arbitrary", "arbitrary"), vmem_limit_bytes=VMEM_LIMIT),
        name="adaln_mod",
    )(c, mod_w, mod_b.reshape(L, 1, N))


def _rope128(blk, cs):
    return blk * cs[:, :LANES] + pltpu.roll(blk, QK_ROPE, 1) * cs[:, LANES:]


def _in_proj_kernel(x_ref, mod_ref, g_ref, w_ref, wifc_ref, wift_ref, cw_ref, cb_ref, qng_ref,
                    kvng_ref, wuq_ref, wkn_ref, wv_ref, cs_ref,
                    q_ref, k_ref, v_ref, o_ref, gc_ref, gr_ref, gt_ref, qa_ref, ka_ref, va_ref,
                    tail_sc):
    tm = x_ref.shape[1]

    @pl.when(pl.program_id(1) == 0)
    def _():
        tail_sc[...] = jnp.zeros_like(tail_sc)

    x = x_ref[0]
    shift = mod_ref[0, 0:1, :]
    scale = mod_ref[0, 1:2, :]
    h = _rms(x, g_ref[...]) * (1.0 + scale) + shift
    hb = h.astype(BF16)

    raw = _dot(hb, w_ref[:, _OFF_QK:_OFF_V])
    tail = tail_sc[...]
    row8 = lax.broadcasted_iota(I32, (SUBLANES, 1), 0)
    acc = cb_ref[...] + cw_ref[CONV_WIDTH - 1:CONV_WIDTH, :] * raw
    for j in range(1, CONV_WIDTH):
        sh = pltpu.roll(raw, j, 0)
        fix = pltpu.roll(tail, j, 0)
        first = jnp.where(row8 < j, fix, sh[:SUBLANES])
        sh = jnp.concatenate([first, sh[SUBLANES:]], axis=0)
        acc = acc + cw_ref[CONV_WIDTH - 1 - j:CONV_WIDTH - j, :] * sh
    tail_sc[...] = raw[tm - SUBLANES:, :]
    qk = _silu(acc)
    q_ref[0] = (qk[:, :M_QK_WIDTH] * (M_QK_DIM ** -0.5)).astype(BF16)
    k_ref[0] = qk[:, M_QK_WIDTH:].astype(BF16)

    v_ref[0] = _dot(hb, w_ref[:, _OFF_V:_OFF_O]).astype(BF16)
    o_ref[0] = _dot(hb, w_ref[:, _OFF_O:_OFF_KR]).astype(BF16)
    gt_ref[0] = _dot(hb, w_ref[:, _OFF_G:_N_MAIN]).astype(BF16)
    gc_ref[0] = _dot(hb, wifc_ref[...])
    gr_ref[0] = _dot_nt(wift_ref[...], hb)

    cs = cs_ref[...]
    kr = _rope128(_dot(hb, w_ref[:, _OFF_KR:_OFF_CQ]), cs).astype(BF16)

    cq = _dot(hb, w_ref[:, _OFF_CQ:_OFF_CKV])
    cqn = _rms(cq, qng_ref[...]).astype(BF16)
    qa = _dot(cqn, wuq_ref[...])
    att_scale = (QK_NOPE + QK_ROPE) ** -0.5 * math.log2(math.e)
    for hh in range(A_HEADS):
        base = hh * A_QK_PAD
        qa_ref[0, hh, :, :QK_NOPE] = (qa[:, base:base + QK_NOPE] * att_scale).astype(BF16)
        roped = _rope128(qa[:, base + QK_NOPE:base + A_QK_PAD], cs)
        qa_ref[0, hh, :, QK_NOPE:] = (roped * att_scale).astype(BF16)

    ckv = _dot(hb, w_ref[:, _OFF_CKV:_OFF_G])
    ckvn = _rms(ckv, kvng_ref[...]).astype(BF16)
    kn = _dot(ckvn, wkn_ref[...]).astype(BF16)
    vv = _dot(ckvn, wv_ref[...]).astype(BF16)
    for hh in range(A_HEADS):
        ka_ref[0, hh, :, :QK_NOPE] = kn[:, hh * QK_NOPE:(hh + 1) * QK_NOPE]
        ka_ref[0, hh, :, QK_NOPE:] = kr
        va_ref[0, hh] = vv[:, hh * V_HEAD:(hh + 1) * V_HEAD]


def _in_proj(x, mod, g, wmain, wifc, wift, conv_w, conv_b, qng, kvng, wuq, wkn, wv, cs):
    B, S, D = x.shape
    tm = TM_PROJ
    ns = S // tm
    tok = lambda c: pl.BlockSpec((1, tm, c), lambda b, s: (b, s, 0))
    head = lambda c: pl.BlockSpec((1, A_HEADS, tm, c), lambda b, s: (b, 0, s, 0))
    out_shape = (
        jax.ShapeDtypeStruct((B, S, M_QK_WIDTH), BF16),
        jax.ShapeDtypeStruct((B, S, M_QK_WIDTH), BF16),
        jax.ShapeDtypeStruct((B, S, M_V_WIDTH), BF16),
        jax.ShapeDtypeStruct((B, S, M_V_WIDTH), BF16),
        jax.ShapeDtypeStruct((B, S, LANES), F32),
        jax.ShapeDtypeStruct((B, SUBLANES, S), F32),
        jax.ShapeDtypeStruct((B, S, 2 * D), BF16),
        jax.ShapeDtypeStruct((B, A_HEADS, S, A_QK_PAD), BF16),
        jax.ShapeDtypeStruct((B, A_HEADS, S, A_QK_PAD), BF16),
        jax.ShapeDtypeStruct((B, A_HEADS, S, V_HEAD), BF16),
    )
    out_specs = (
        tok(M_QK_WIDTH), tok(M_QK_WIDTH), tok(M_V_WIDTH), tok(M_V_WIDTH), tok(LANES),
        pl.BlockSpec((1, SUBLANES, tm), lambda b, s: (b, 0, s)),
        tok(2 * D), head(A_QK_PAD), head(A_QK_PAD), head(V_HEAD),
    )
    in_specs = [
        tok(D),
        pl.BlockSpec((1, N_MOD, D), lambda b, s: (b, 0, 0)),
        _const_spec((1, D)),
        _const_spec(wmain.shape), _const_spec(wifc.shape), _const_spec(wift.shape),
        _const_spec(conv_w.shape), _const_spec(conv_b.shape),
        _const_spec(qng.shape), _const_spec(kvng.shape),
        _const_spec(wuq.shape), _const_spec(wkn.shape), _const_spec(wv.shape),
        pl.BlockSpec((tm, 2 * LANES), lambda b, s: (s, 0)),
    ]
    return pl.pallas_call(
        _in_proj_kernel,
        grid=(B, ns),
        in_specs=in_specs,
        out_specs=out_specs,
        out_shape=out_shape,
        scratch_shapes=[pltpu.VMEM((SUBLANES, 2 * M_QK_WIDTH), F32)],
        compiler_params=pltpu.CompilerParams(
            dimension_semantics=("arbitrary", "arbitrary"), vmem_limit_bytes=VMEM_LIMIT),
        name="in_proj",
    )(x, mod, g, wmain, wifc, wift, conv_w, conv_b, qng, kvng, wuq, wkn, wv, cs)


def _log_sigmoid(x):
    return jnp.minimum(x, 0.0) - jnp.log(1.0 + jnp.exp(-jnp.abs(x)))


def _mlstm_kernel(q_ref, k_ref, v_ref, gc_ref, gr_ref, bc_ref, br_ref, y_ref, c_sc, n_sc, m_sc):
    L = q_ref.shape[1]

    @pl.when(pl.program_id(1) == 0)
    def _():
        c_sc[...] = jnp.zeros_like(c_sc)
        n_sc[...] = jnp.zeros_like(n_sc)
        m_sc[...] = jnp.zeros_like(m_sc)

    rows = lax.broadcasted_iota(I32, (L, L), 0)
    cols = lax.broadcasted_iota(I32, (L, L), 1)
    causal = cols <= rows
    tri = jnp.where(causal, 1.0, 0.0).astype(BF16)
    tri_t = jnp.where(rows <= cols, 1.0, 0.0).astype(BF16)

    gcol = gc_ref[0] + bc_ref[...]
    grow = gr_ref[0] + br_ref[:, :1]
    lf_col = _log_sigmoid(gcol)
    lf_row = _log_sigmoid(grow)
    ch, cl = _split_bf16(lf_col)
    b_col = _dot(tri, ch) + _dot(tri, cl)
    rh, rl = _split_bf16(lf_row)
    b_row = _dot(rh, tri_t) + _dot(rl, tri_t)

    for hh in range(M_HEADS):
        q = q_ref[0, :, hh * M_QK_DIM:(hh + 1) * M_QK_DIM]
        k = k_ref[0, :, hh * M_QK_DIM:(hh + 1) * M_QK_DIM]
        v = v_ref[0, :, hh * M_V_DIM:(hh + 1) * M_V_DIM]
        li_c = gcol[:, hh:hh + 1]
        b_c = b_col[:, M_HEADS + hh:M_HEADS + hh + 1]
        li_r = grow[hh:hh + 1, :]
        b_r = b_row[M_HEADS + hh:M_HEADS + hh + 1, :]
        m_prev = m_sc[hh][:, :1]
        c_prev = c_sc[hh]
        n_prev = n_sc[hh]

        dmat = jnp.where(causal, b_c - b_r + li_r, -jnp.inf)
        m_inter = b_c + m_prev
        m_t = jnp.maximum(jnp.max(dmat, axis=-1, keepdims=True), m_inter)
        s = _dot_nt(q, k) * jnp.exp(dmat - m_t)
        w_inter = jnp.exp(m_inter - m_t)
        num = _dot(s.astype(BF16), v) + w_inter * _dot(q, c_prev.astype(BF16))
        qn = jnp.sum(q.astype(F32) * n_prev, axis=-1, keepdims=True)
        den = jnp.sum(s, axis=-1, keepdims=True) + w_inter * qn
        hout = num / jnp.maximum(jnp.abs(den), jnp.exp(-m_t))
        y_ref[0, :, hh * M_V_DIM:(hh + 1) * M_V_DIM] = hout.astype(y_ref.dtype)

        b_last = b_c[L - 1:L, :]
        a_c = b_last - b_c + li_c
        m_new = jnp.maximum(b_last + m_prev, jnp.max(a_c, axis=0, keepdims=True))
        decay = jnp.exp(b_last + m_prev - m_new)
        wk = jnp.exp(a_c - m_new)
        wv = (wk * v.astype(F32)).astype(BF16)
        c_sc[hh] = decay * c_prev + _dot_tn(k, wv)
        n_sc[hh] = decay * n_prev + jnp.sum(wk * k.astype(F32), axis=0, keepdims=True)
        m_sc[hh] = jnp.broadcast_to(m_new, (1, LANES))


def _mlstm(q, k, v, gc, gr, bias_c, bias_r):
    B, S, _ = q.shape
    L = MLSTM_CHUNK
    tok = lambda c: pl.BlockSpec((1, L, c), lambda b, s: (b, s, 0))
    return pl.pallas_call(
        _mlstm_kernel,
        grid=(B, S // L),
        in_specs=[
            tok(M_QK_WIDTH), tok(M_QK_WIDTH), tok(M_V_WIDTH), tok(LANES),
            pl.BlockSpec((1, SUBLANES, L), lambda b, s: (b, 0, s)),
            _const_spec(bias_c.shape), _const_spec(bias_r.shape),
        ],
        out_specs=tok(M_V_WIDTH),
        out_shape=jax.ShapeDtypeStruct((B, S, M_V_WIDTH), BF16),
        scratch_shapes=[
            pltpu.VMEM((M_HEADS, M_QK_DIM, M_V_DIM), F32),
            pltpu.VMEM((M_HEADS, 1, M_QK_DIM), F32),
            pltpu.VMEM((M_HEADS, 1, LANES), F32),
        ],
        compiler_params=pltpu.CompilerParams(
            dimension_semantics=("arbitrary", "arbitrary"), vmem_limit_bytes=VMEM_LIMIT),
        name="mlstm",
    )(q, k, v, gc, gr, bias_c, bias_r)


def _attn_kernel(q_ref, k_ref, v_ref, o_ref, m_sc, l_sc, acc_sc):
    nh = q_ref.shape[1]
    tq = q_ref.shape[2]
    qi = pl.program_id(2)
    nblk = TK // LANES

    m_sc[...] = jnp.full_like(m_sc, -jnp.inf)
    l_sc[...] = jnp.zeros_like(l_sc)
    acc_sc[...] = jnp.zeros_like(acc_sc)

    def update(hh, start, mask):
        q = q_ref[0, hh]
        k = k_ref[0, hh, pl.ds(start, TK), :]
        v = v_ref[0, hh, pl.ds(start, TK), :]
        s = _dot_nt(q, k)
        if mask is not None:
            s = jnp.where(mask, s, -jnp.inf)
        blocks = [s[:, c * LANES:(c + 1) * LANES] for c in range(nblk)]
        mx = blocks[0]
        for blk in blocks[1:]:
            mx = jnp.maximum(mx, blk)
        m_prev = m_sc[hh]
        m_cur = jnp.maximum(m_prev, jnp.max(mx, axis=-1, keepdims=True))
        alpha = jnp.exp2(m_prev - m_cur)
        ps = [jnp.exp2(blk - m_cur) for blk in blocks]
        psum = ps[0]
        for pb in ps[1:]:
            psum = psum + pb
        l_sc[hh] = alpha * l_sc[hh] + psum
        p = jnp.concatenate(ps, axis=1).astype(BF16)
        acc_sc[hh] = alpha * acc_sc[hh] + _dot(p, v)
        m_sc[hh] = m_cur

    def body(j, carry):
        for hh in range(nh):
            update(hh, pl.multiple_of(j * TK, TK), None)
        return carry

    lax.fori_loop(0, qi * (tq // TK), body, 0)

    for jj in range(tq // TK):
        qc = lax.broadcasted_iota(I32, (tq, TK), 0) // CHUNK
        kc = (lax.broadcasted_iota(I32, (tq, TK), 1) + jj * TK) // CHUNK
        for hh in range(nh):
            update(hh, pl.multiple_of(qi * tq + jj * TK, TK), kc <= qc)

    for hh in range(nh):
        l = jnp.sum(l_sc[hh], axis=-1, keepdims=True)
        o_ref[0, :, hh * V_HEAD:(hh + 1) * V_HEAD] = (acc_sc[hh] / l).astype(o_ref.dtype)


def _attention(qa, ka, va):
    B, H, S, _ = qa.shape
    nh = ATTN_HEADS_PER_STEP
    return pl.pallas_call(
        _attn_kernel,
        grid=(B, H // nh, S // TQ),
        in_specs=[
            pl.BlockSpec((1, nh, TQ, A_QK_PAD), lambda b, h, i: (b, h, i, 0)),
            pl.BlockSpec((1, nh, S, A_QK_PAD), lambda b, h, i: (b, h, 0, 0),
                         pipeline_mode=pl.Buffered(1)),
            pl.BlockSpec((1, nh, S, V_HEAD), lambda b, h, i: (b, h, 0, 0),
                         pipeline_mode=pl.Buffered(1)),
        ],
        out_specs=pl.BlockSpec((1, TQ, nh * V_HEAD), lambda b, h, i: (b, i, h)),
        out_shape=jax.ShapeDtypeStruct((B, S, H * V_HEAD), BF16),
        scratch_shapes=[
            pltpu.VMEM((nh, TQ, LANES), F32), pltpu.VMEM((nh, TQ, LANES), F32),
            pltpu.VMEM((nh, TQ, V_HEAD), F32),
        ],
        compiler_params=pltpu.CompilerParams(
            dimension_semantics=("arbitrary", "arbitrary", "arbitrary"),
            vmem_limit_bytes=VMEM_LIMIT),
        name="mla_attention",
    )(qa, ka, va)


_ROUTE_LANE0 = N_GROUPS


def _merge_kernel(x_ref, ym_ref, o_ref, ya_ref, gt_ref, mod_ref, g2_ref, wm_ref, wa_ref, wo_ref,
                  wrh_ref, wrl_ref, br_ref,
                  xo_ref, h2_ref, ri_ref, rw_ref, cnt_ref, carry_sc):
    tm = x_ref.shape[1]
    D = x_ref.shape[2]

    @pl.when((pl.program_id(0) == 0) & (pl.program_id(1) == 0))
    def _():
        carry_sc[...] = jnp.zeros_like(carry_sc)

    ym = ym_ref[0].astype(F32) * _sigmoid(o_ref[0].astype(F32))
    bm = _dot(ym.astype(BF16), wm_ref[...])
    ba = _dot(ya_ref[0], wa_ref[...])
    gt = gt_ref[0].astype(F32)
    merged = _sigmoid(gt[:, :D]) * bm + _sigmoid(gt[:, D:]) * ba
    out = _dot(merged.astype(BF16), wo_ref[...])
    xn = x_ref[0] + mod_ref[0, 2:3, :] * out
    xo_ref[0] = xn

    h2 = _rms(xn, g2_ref[...]) * (1.0 + mod_ref[0, 4:5, :]) + mod_ref[0, 3:4, :]
    h2_ref[0] = h2

    hh, hl = _split_bf16(h2)
    logits = _dot(hh, wrh_ref[...]) + _dot(hl, wrh_ref[...]) + _dot(hh, wrl_ref[...]) + br_ref[...]
    lane = lax.broadcasted_iota(I32, (tm, LANES), 1).astype(F32)
    big = float(LANES)
    is_grp = lane < N_GROUPS
    gl = jnp.where(is_grp, logits, -jnp.inf)
    gmax = jnp.max(gl, axis=-1, keepdims=True)
    grp = jnp.min(jnp.where(gl == gmax, lane, big), axis=-1, keepdims=True)
    p_grp = 1.0 / jnp.sum(jnp.where(is_grp, jnp.exp(logits - gmax), 0.0), axis=-1, keepdims=True)
    lo = _ROUTE_LANE0 + grp * EXPERTS_PER_GROUP
    el = jnp.where((lane >= lo) & (lane < lo + EXPERTS_PER_GROUP), logits, -jnp.inf)
    m1 = jnp.max(el, axis=-1, keepdims=True)
    i1 = jnp.min(jnp.where(el == m1, lane, big), axis=-1, keepdims=True)
    el2 = jnp.where(lane == i1, -jnp.inf, el)
    m2 = jnp.max(el2, axis=-1, keepdims=True)
    i2 = jnp.min(jnp.where(el2 == m2, lane, big), axis=-1, keepdims=True)
    r = jnp.exp(m2 - m1)
    w1 = p_grp / (1.0 + r)
    w2 = p_grp * r / (1.0 + r)

    hot1 = lane == i1
    hot2 = lane == i2
    onehot = jnp.where(hot1 | hot2, 1.0, 0.0)
    rows = lax.broadcasted_iota(I32, (tm, tm), 0)
    cols = lax.broadcasted_iota(I32, (tm, tm), 1)
    strict = jnp.where(cols < rows, 1.0, 0.0).astype(BF16)
    before = _dot(strict, onehot.astype(BF16)) + carry_sc[...]
    r1 = jnp.sum(jnp.where(hot1, before, 0.0), axis=-1, keepdims=True)
    r2 = jnp.sum(jnp.where(hot2, before, 0.0), axis=-1, keepdims=True)
    carry = carry_sc[...] + jnp.sum(onehot, axis=0, keepdims=True)
    carry_sc[...] = carry
    cnt_ref[...] = carry

    route = jnp.where(lane == 0, i1 - _ROUTE_LANE0,
                      jnp.where(lane == 1, i2 - _ROUTE_LANE0,
                                jnp.where(lane == 2, r1, jnp.where(lane == 3, r2, 0.0))))
    ri_ref[0] = route.astype(I32)
    rw_ref[0] = jnp.where(lane == 0, w1, jnp.where(lane == 1, w2, 0.0))


def _merge(x, ym, o, ya, gt, mod, g2, wm, wa, wo, wrh, wrl, br):
    B, S, D = x.shape
    tm = TM_PROJ
    tok = lambda c: pl.BlockSpec((1, tm, c), lambda b, s: (b, s, 0))
    out_shape = (
        jax.ShapeDtypeStruct((B, S, D), F32),
        jax.ShapeDtypeStruct((B, S, D), F32),
        jax.ShapeDtypeStruct((B, S, LANES), I32),
        jax.ShapeDtypeStruct((B, S, LANES), F32),
        jax.ShapeDtypeStruct((1, LANES), F32),
    )
    out_specs = (tok(D), tok(D), tok(LANES), tok(LANES),
                 pl.BlockSpec((1, LANES), lambda b, s: (0, 0)))
    in_specs = [
        tok(D), tok(M_V_WIDTH), tok(M_V_WIDTH), tok(A_HEADS * V_HEAD), tok(2 * D),
        pl.BlockSpec((1, N_MOD, D), lambda b, s: (b, 0, 0)),
        _const_spec((1, D)),
        _const_spec(wm.shape), _const_spec(wa.shape), _const_spec(wo.shape),
        _const_spec(wrh.shape), _const_spec(wrl.shape), _const_spec(br.shape),
    ]
    return pl.pallas_call(
        _merge_kernel,
        grid=(B, S // tm),
        in_specs=in_specs,
        out_specs=out_specs,
        out_shape=out_shape,
        scratch_shapes=[pltpu.VMEM((1, LANES), F32)],
        compiler_params=pltpu.CompilerParams(
            dimension_semantics=("arbitrary", "arbitrary"), vmem_limit_bytes=VMEM_LIMIT),
        name="merge_router",
    )(x, ym, o, ya, gt, mod, g2, wm, wa, wo, wrh, wrl, br)


def _row_copy(src, si, dst, di, sem):
    return pltpu.make_async_copy(src.at[pl.ds(si, 1), :], dst.at[pl.ds(di, 1), :], sem)


def _dispatch_kernel(pstart_ref, padlo_ref, pend_ref, eid_ref, rank_ref, h_ref, xg_ref, dest_ref,
                     zero_sc, sem, zsem):
    n = eid_ref.shape[0]
    cap = xg_ref.shape[0]
    step = pl.program_id(0)

    @pl.when(step == 0)
    def _():
        zero_sc[...] = jnp.zeros_like(zero_sc)

        def fill(lo, hi):
            def one(r, c):
                _row_copy(zero_sc, 0, xg_ref, r, zsem).start()
                return c
            lax.fori_loop(lo, hi, one, 0)

        def per_expert(e, c):
            fill(padlo_ref[e], pend_ref[e])
            return c
        lax.fori_loop(0, N_EXPERTS, per_expert, 0)
        fill(pend_ref[N_EXPERTS - 1], cap)

    def issue(a, c):
        dst = pstart_ref[eid_ref[a]] + rank_ref[a]
        dest_ref[a] = dst
        _row_copy(h_ref, a // TOP_K, xg_ref, dst, sem).start()
        return c
    lax.fori_loop(0, n, issue, 0, unroll=4)

    def drain(a, c):
        _row_copy(h_ref, 0, xg_ref, 0, sem).wait()
        return c
    lax.fori_loop(0, n, drain, 0, unroll=8)

    @pl.when(step == pl.num_programs(0) - 1)
    def _():
        n_fill = cap - pl.num_programs(0) * n

        def zdrain(a, c):
            _row_copy(zero_sc, 0, xg_ref, 0, zsem).wait()
            return c
        lax.fori_loop(0, n_fill, zdrain, 0)


def _dispatch(pstart, padlo, pend, eid, rank, h2, cap):
    T, D = h2.shape
    n = DISPATCH_ROWS
    smem_blk = pl.BlockSpec((n,), lambda i, *_: (i,), memory_space=pltpu.SMEM)
    grid_spec = pltpu.PrefetchScalarGridSpec(
        num_scalar_prefetch=3,
        grid=(T * TOP_K // n,),
        in_specs=[smem_blk, smem_blk, pl.BlockSpec((n // TOP_K, D), lambda i, *_: (i, 0))],
        out_specs=[pl.BlockSpec(memory_space=pl.ANY), smem_blk],
        scratch_shapes=[pltpu.VMEM((SUBLANES, D), F32), pltpu.SemaphoreType.DMA,
                        pltpu.SemaphoreType.DMA],
    )
    return pl.pallas_call(
        _dispatch_kernel,
        grid_spec=grid_spec,
        out_shape=[jax.ShapeDtypeStruct((cap, D), F32),
                   jax.ShapeDtypeStruct((T * TOP_K,), I32)],
        compiler_params=pltpu.CompilerParams(
            dimension_semantics=("arbitrary",), vmem_limit_bytes=VMEM_LIMIT),
        name="moe_dispatch",
    )(pstart, padlo, pend, eid, rank, h2)


def _expert_kernel(blk_e_ref, nused_ref, x_ref, wgu_ref, wd_ref, o_ref):
    i = pl.program_id(0)

    @pl.when(i < nused_ref[0])
    def _():
        xb = x_ref[...].astype(BF16)
        gu = _dot(xb, wgu_ref[0])
        a = _silu(gu[:, :D_EXPERT]) * gu[:, D_EXPERT:]
        o_ref[...] = _dot(a.astype(BF16), wd_ref[0])

    @pl.when(i >= nused_ref[0])
    def _():
        o_ref[...] = jnp.zeros_like(o_ref)


def _experts(blk_e, nused, xg, wgu, wd):
    cap, D = xg.shape
    R = MOE_ROWS
    grid_spec = pltpu.PrefetchScalarGridSpec(
        num_scalar_prefetch=2,
        grid=(cap // R,),
        in_specs=[
            pl.BlockSpec((R, D), lambda i, be, nu: (i, 0)),
            pl.BlockSpec((1, D, 2 * D_EXPERT), lambda i, be, nu: (be[i], 0, 0)),
            pl.BlockSpec((1, D_EXPERT, D), lambda i, be, nu: (be[i], 0, 0)),
        ],
        out_specs=pl.BlockSpec((R, D), lambda i, be, nu: (i, 0)),
    )
    return pl.pallas_call(
        _expert_kernel,
        grid_spec=grid_spec,
        out_shape=jax.ShapeDtypeStruct((cap, D), F32),
        compiler_params=pltpu.CompilerParams(
            dimension_semantics=("arbitrary",), vmem_limit_bytes=VMEM_LIMIT),
        name="moe_experts",
    )(blk_e, nused, xg, wgu, wd)


def _combine_kernel(dest_ref, og_ref, w_ref, x_ref, mod_ref, fg_ref, o_ref, buf_sc, sem, *,
                    final_norm):
    tb = x_ref.shape[1]

    def issue(t, c):
        for kk in range(TOP_K):
            pltpu.make_async_copy(og_ref.at[pl.ds(dest_ref[t * TOP_K + kk], 1), :],
                                  buf_sc.at[kk, pl.ds(t, 1), :], sem).start()
        return c
    lax.fori_loop(0, tb, issue, 0, unroll=2)

    def drain(t, c):
        for kk in range(TOP_K):
            pltpu.make_async_copy(og_ref.at[pl.ds(0, 1), :],
                                  buf_sc.at[kk, pl.ds(0, 1), :], sem).wait()
        return c
    lax.fori_loop(0, tb, drain, 0, unroll=4)

    w = w_ref[0]
    y = w[:, 0:1] * buf_sc[0] + w[:, 1:2] * buf_sc[1]
    xo = x_ref[0] + mod_ref[0, 5:6, :] * y
    if final_norm:
        xo = _rms(xo, fg_ref[...])
    o_ref[0] = xo


def _combine(dest, outg, rw, x, mod, fg, final_norm):
    B, S, D = x.shape
    tb = TB_COMBINE
    ns = S // tb
    return pl.pallas_call(
        functools.partial(_combine_kernel, final_norm=final_norm),
        grid=(B, ns),
        in_specs=[
            pl.BlockSpec((tb * TOP_K,), lambda b, s: (b * ns + s,), memory_space=pltpu.SMEM),
            pl.BlockSpec(memory_space=pl.ANY),
            pl.BlockSpec((1, tb, LANES), lambda b, s: (b, s, 0)),
            pl.BlockSpec((1, tb, D), lambda b, s: (b, s, 0)),
            pl.BlockSpec((1, N_MOD, D), lambda b, s: (b, 0, 0)),
            pl.BlockSpec((1, D), lambda b, s: (0, 0)),
        ],
        out_specs=pl.BlockSpec((1, tb, D), lambda b, s: (b, s, 0)),
        out_shape=jax.ShapeDtypeStruct((B, S, D), F32),
        scratch_shapes=[pltpu.VMEM((TOP_K, tb, D), F32), pltpu.SemaphoreType.DMA],
        compiler_params=pltpu.CompilerParams(
            dimension_semantics=("arbitrary", "arbitrary"), vmem_limit_bytes=VMEM_LIMIT),
        name="moe_combine",
    )(dest, outg, rw, x, mod, fg)


def _rotate_half_cols(w):
    half = w.shape[-1] // 2
    return jnp.concatenate([-w[..., half:], w[..., :half]], axis=-1)


def _pack_layer_weights(l, w_in, w_uq, w_ukv, w_group, w_router, b_group, b_router,
                        w_expert_gate, w_expert_up, igate_b, fgate_b):
    D = w_in.shape[1]
    sizes = (M_QK_WIDTH, M_QK_WIDTH, M_V_WIDTH, M_V_WIDTH, M_HEADS, M_HEADS, Q_LORA, KV_LORA,
             QK_ROPE, 2 * D)
    pts = [sum(sizes[:i + 1]) for i in range(len(sizes) - 1)]
    wq, wk, wv, wo, wi, wf, wcq, wckv, wkr, wg = jnp.split(w_in[l], pts, axis=1)
    wmain = jnp.concatenate([wq, wk, wv, wo, wkr, _rotate_half_cols(wkr), wcq, wckv, wg],
                            axis=1).astype(BF16)
    wif = jnp.concatenate([wi, wf], axis=1)
    wifc = jnp.pad(wif, ((0, 0), (0, LANES - 2 * M_HEADS))).astype(BF16)
    wift = wif.T.astype(BF16)

    uq = w_uq[l].reshape(Q_LORA, A_HEADS, QK_NOPE + QK_ROPE)
    uq_rope = uq[:, :, QK_NOPE:]
    wuq = jnp.concatenate([uq[:, :, :QK_NOPE], uq_rope, _rotate_half_cols(uq_rope)], axis=-1)
    wuq = wuq.reshape(Q_LORA, A_HEADS * A_QK_PAD).astype(BF16)
    ukv = w_ukv[l].reshape(KV_LORA, A_HEADS, QK_NOPE + V_HEAD)
    wkn = ukv[:, :, :QK_NOPE].reshape(KV_LORA, A_HEADS * QK_NOPE).astype(BF16)
    wvv = ukv[:, :, QK_NOPE:].reshape(KV_LORA, A_HEADS * V_HEAD).astype(BF16)

    wr = jnp.concatenate([w_group[l], w_router[l]], axis=1)
    wr = jnp.pad(wr, ((0, 0), (0, LANES - wr.shape[1])))
    wrh = wr.astype(BF16)
    wrl = (wr - wrh.astype(F32)).astype(BF16)
    br = jnp.concatenate([b_group[l], b_router[l]])
    br = jnp.pad(br, (0, LANES - br.shape[0])).reshape(1, LANES)

    wgu = jnp.concatenate([w_expert_gate[l], w_expert_up[l]], axis=-1).astype(BF16)

    gb = jnp.concatenate([igate_b[l], fgate_b[l]])
    bias_c = jnp.pad(gb, (0, LANES - gb.shape[0])).reshape(1, LANES)
    bias_r = jnp.broadcast_to(gb[:, None], (2 * M_HEADS, LANES))
    return wmain, wifc, wift, wuq, wkn, wvv, wrh, wrl, br, wgu, bias_c, bias_r


def kernel(x, c, mod_w, mod_b, norm1_g, w_in, conv_w, conv_b, igate_b, fgate_b, q_norm_g,
           kv_norm_g, w_uq, w_ukv, w_branch_m, w_branch_a, w_out, norm2_g, w_group, b_group,
           w_router, b_router, w_expert_gate, w_expert_up, w_expert_down, final_norm_g):
    B, S, D = x.shape
    L = mod_w.shape[0]
    T = B * S
    assert S % TQ == 0 and S % MLSTM_CHUNK == 0 and S % TM_PROJ == 0 and S % TB_COMBINE == 0
    assert TQ % TK == 0 and TQ % CHUNK == 0 and (T * TOP_K) % DISPATCH_ROWS == 0

    pos = jnp.arange(S, dtype=F32)
    inv_freq = 1.0 / (ROPE_THETA ** (jnp.arange(0, QK_ROPE, 2, dtype=F32) / QK_ROPE))
    ang = pos[:, None] * inv_freq[None, :]
    cos = jnp.cos(ang)
    sin = jnp.sin(ang)
    z = jnp.zeros((S, QK_ROPE), F32)
    cs = jnp.concatenate([cos, cos, z, sin, sin, z], axis=1)

    mod_all = _modulation(c, mod_w, mod_b).reshape(L, B, N_MOD, D)

    cap = T * TOP_K + N_EXPERTS * MOE_ROWS
    n_blocks = cap // MOE_ROWS

    for l in range(L):
        (wmain, wifc, wift, wuq, wkn, wvv, wrh, wrl, br, wgu, bias_c, bias_r) = _pack_layer_weights(
            l, w_in, w_uq, w_ukv, w_group, w_router, b_group, b_router, w_expert_gate,
            w_expert_up, igate_b, fgate_b)
        mod = mod_all[l]

        q, k, v, o, gc, gr, gt, qa, ka, va = _in_proj(
            x, mod, norm1_g[l].reshape(1, D), wmain, wifc, wift, conv_w[l],
            conv_b[l].reshape(1, -1), q_norm_g[l].reshape(1, -1), kv_norm_g[l].reshape(1, -1),
            wuq, wkn, wvv, cs)
        ym = _mlstm(q, k, v, gc, gr, bias_c, bias_r)
        ya = _attention(qa, ka, va)
        xn, h2, ri, rw, cnt = _merge(
            x, ym, o, ya, gt, mod, norm2_g[l].reshape(1, D), w_branch_m[l].astype(BF16),
            w_branch_a[l].astype(BF16), w_out[l].astype(BF16), wrh, wrl, br)

        counts = cnt[0, _ROUTE_LANE0:_ROUTE_LANE0 + N_EXPERTS].astype(I32)
        padded = (counts + MOE_ROWS - 1) // MOE_ROWS * MOE_ROWS
        pend = jnp.cumsum(padded).astype(I32)
        pstart = pend - padded
        padlo = pstart + counts
        blk_start = jnp.arange(n_blocks, dtype=I32) * MOE_ROWS
        blk_e = jnp.minimum(jnp.sum(pend[None, :] <= blk_start[:, None], axis=1),
                            N_EXPERTS - 1).astype(I32)
        nused = (pend[N_EXPERTS - 1:] // MOE_ROWS).astype(I32)
        eid = ri[:, :, 0:TOP_K].reshape(T * TOP_K)
        rank = ri[:, :, TOP_K:2 * TOP_K].reshape(T * TOP_K)

        xg, dest = _dispatch(pstart, padlo, pend, eid, rank, h2.reshape(T, D), cap)
        outg = _experts(blk_e, nused, xg, wgu, w_expert_down[l].astype(BF16))
        x = _combine(dest, outg, rw, xn, mod, final_norm_g.reshape(1, D),
                     final_norm=(l == L - 1))
    return x
```

```python
import functools
import math

import jax
import jax.numpy as jnp
from jax import lax
from jax.experimental import pallas as pl
from jax.experimental.pallas import tpu as pltpu

F32 = jnp.float32
BF16 = jnp.bfloat16
I32 = jnp.int32

EPS = 1e-6
CHUNK = 64
N_MOD = 6

M_HEADS = 4
M_QK_DIM = 128
M_V_DIM = 256
M_QK_WIDTH = M_HEADS * M_QK_DIM
M_V_WIDTH = M_HEADS * M_V_DIM
CONV_WIDTH = 4

A_HEADS = 8
Q_LORA = 384
KV_LORA = 256
QK_NOPE = 128
QK_ROPE = 64
V_HEAD = 128
A_QK_PAD = 256
ROPE_THETA = 10000.0

N_GROUPS = 4
EXPERTS_PER_GROUP = 8
N_EXPERTS = N_GROUPS * EXPERTS_PER_GROUP
PAIRS_PER_GROUP = EXPERTS_PER_GROUP * (EXPERTS_PER_GROUP - 1) // 2
N_BUCKETS = N_GROUPS * PAIRS_PER_GROUP
TOP_K = 2
D_EXPERT = 256

LANES = 128
SUBLANES = 8

TM_PROJ = 512
MLSTM_CHUNK = 256
TQ = 512
TK = 512
ATTN_HEADS_PER_STEP = 4
MOE_ROWS = 128
DISPATCH_ROWS = 1024
TB_COMBINE = 256

VMEM_LIMIT = 56 * 1024 * 1024

_OFF_QK = 0
_OFF_V = _OFF_QK + 2 * M_QK_WIDTH
_OFF_O = _OFF_V + M_V_WIDTH
_OFF_KR = _OFF_O + M_V_WIDTH
_OFF_CQ = _OFF_KR + 2 * QK_ROPE
_OFF_CKV = _OFF_CQ + Q_LORA
_OFF_G = _OFF_CKV + KV_LORA
_N_MAIN = _OFF_G + 2 * 1024


def _dot(a, b):
    return jnp.dot(a, b, preferred_element_type=F32)


def _dot_nt(a, b):
    return lax.dot_general(a, b, (((1,), (1,)), ((), ())), preferred_element_type=F32)


def _dot_tn(a, b):
    return lax.dot_general(a, b, (((0,), (0,)), ((), ())), preferred_element_type=F32)


def _split_bf16(x):
    hi = x.astype(BF16)
    lo = (x - hi.astype(F32)).astype(BF16)
    return hi, lo


def _sigmoid(x):
    return 1.0 / (1.0 + jnp.exp(-x))


def _silu(x):
    return x * _sigmoid(x)


def _rms(x, g):
    return x * lax.rsqrt(jnp.mean(x * x, axis=-1, keepdims=True) + EPS) * g


def _const_spec(shape):
    nd = len(shape)
    return pl.BlockSpec(shape, lambda *_: (0,) * nd, pipeline_mode=pl.Buffered(1))


def _mod_kernel(c_ref, w_ref, b_ref, o_ref):
    c = c_ref[...]
    cond = _silu(c)
    ch, cl = _split_bf16(cond)
    wh, wl = _split_bf16(w_ref[0])
    o_ref[0] = _dot(ch, wh) + _dot(cl, wh) + _dot(ch, wl) + b_ref[0]


def _modulation(c, mod_w, mod_b):
    L, D, N = mod_w.shape
    B = c.shape[0]
    tn = 1536
    return pl.pallas_call(
        _mod_kernel,
        grid=(L, N // tn),
        in_specs=[
            pl.BlockSpec((B, D), lambda l, j: (0, 0)),
            pl.BlockSpec((1, D, tn), lambda l, j: (l, 0, j)),
            pl.BlockSpec((1, 1, tn), lambda l, j: (l, 0, j)),
        ],
        out_specs=pl.BlockSpec((1, B, tn), lambda l, j: (l, 0, j)),
        out_shape=jax.ShapeDtypeStruct((L, B, N), F32),
        compiler_params=pltpu.CompilerParams(
            dimension_semantics=("arbitrary", "arbitrary"), vmem_limit_bytes=VMEM_LIMIT),
        name="adaln_mod",
    )(c, mod_w, mod_b.reshape(L, 1, N))


def _rope128(blk, cs):
    return blk * cs[:, :LANES] + pltpu.roll(blk, QK_ROPE, 1) * cs[:, LANES:]


def _in_proj_kernel(x_ref, mod_ref, g_ref, w_ref, wifc_ref, wift_ref, cw_ref, cb_ref, qng_ref,
                    kvng_ref, wuq_ref, wkn_ref, wv_ref, cs_ref,
                    q_ref, k_ref, v_ref, o_ref, gc_ref, gr_ref, gt_ref, qa_ref, ka_ref, va_ref,
                    tail_sc):
    tm = x_ref.shape[1]

    @pl.when(pl.program_id(1) == 0)
    def _():
        tail_sc[...] = jnp.zeros_like(tail_sc)

    x = x_ref[0]
    shift = mod_ref[0, 0:1, :]
    scale = mod_ref[0, 1:2, :]
    h = _rms(x, g_ref[...]) * (1.0 + scale) + shift
    hb = h.astype(BF16)

    raw = _dot(hb, w_ref[:, _OFF_QK:_OFF_V])
    tail = tail_sc[...]
    row8 = lax.broadcasted_iota(I32, (SUBLANES, 1), 0)
    acc = cb_ref[...] + cw_ref[CONV_WIDTH - 1:CONV_WIDTH, :] * raw
    for j in range(1, CONV_WIDTH):
        sh = pltpu.roll(raw, j, 0)
        fix = pltpu.roll(tail, j, 0)
        first = jnp.where(row8 < j, fix, sh[:SUBLANES])
        sh = jnp.concatenate([first, sh[SUBLANES:]], axis=0)
        acc = acc + cw_ref[CONV_WIDTH - 1 - j:CONV_WIDTH - j, :] * sh
    tail_sc[...] = raw[tm - SUBLANES:, :]
    qk = _silu(acc)
    q_ref[0] = (qk[:, :M_QK_WIDTH] * (M_QK_DIM ** -0.5)).astype(BF16)
    k_ref[0] = qk[:, M_QK_WIDTH:].astype(BF16)

    v_ref[0] = _dot(hb, w_ref[:, _OFF_V:_OFF_O]).astype(BF16)
    o_ref[0] = _dot(hb, w_ref[:, _OFF_O:_OFF_KR]).astype(BF16)
    gt_ref[0] = _dot(hb, w_ref[:, _OFF_G:_N_MAIN]).astype(BF16)
    gc_ref[0] = _dot(hb, wifc_ref[...])
    gr_ref[0] = _dot_nt(wift_ref[...], hb)

    cs = cs_ref[...]
    kr = _rope128(_dot(hb, w_ref[:, _OFF_KR:_OFF_CQ]), cs).astype(BF16)

    cq = _dot(hb, w_ref[:, _OFF_CQ:_OFF_CKV])
    cqn = _rms(cq, qng_ref[...]).astype(BF16)
    qa = _dot(cqn, wuq_ref[...])
    att_scale = (QK_NOPE + QK_ROPE) ** -0.5 * math.log2(math.e)
    for hh in range(A_HEADS):
        base = hh * A_QK_PAD
        qa_ref[0, hh, :, :QK_NOPE] = (qa[:, base:base + QK_NOPE] * att_scale).astype(BF16)
        roped = _rope128(qa[:, base + QK_NOPE:base + A_QK_PAD], cs)
        qa_ref[0, hh, :, QK_NOPE:] = (roped * att_scale).astype(BF16)

    ckv = _dot(hb, w_ref[:, _OFF_CKV:_OFF_G])
    ckvn = _rms(ckv, kvng_ref[...]).astype(BF16)
    kn = _dot(ckvn, wkn_ref[...]).astype(BF16)
    vv = _dot(ckvn, wv_ref[...]).astype(BF16)
    for hh in range(A_HEADS):
        ka_ref[0, hh, :, :QK_NOPE] = kn[:, hh * QK_NOPE:(hh + 1) * QK_NOPE]
        ka_ref[0, hh, :, QK_NOPE:] = kr
        va_ref[0, hh] = vv[:, hh * V_HEAD:(hh + 1) * V_HEAD]


def _in_proj(x, mod, g, wmain, wifc, wift, conv_w, conv_b, qng, kvng, wuq, wkn, wv, cs):
    B, S, D = x.shape
    tm = TM_PROJ
    ns = S // tm
    tok = lambda c: pl.BlockSpec((1, tm, c), lambda b, s: (b, s, 0))
    head = lambda c: pl.BlockSpec((1, A_HEADS, tm, c), lambda b, s: (b, 0, s, 0))
    out_shape = (
        jax.ShapeDtypeStruct((B, S, M_QK_WIDTH), BF16),
        jax.ShapeDtypeStruct((B, S, M_QK_WIDTH), BF16),
        jax.ShapeDtypeStruct((B, S, M_V_WIDTH), BF16),
        jax.ShapeDtypeStruct((B, S, M_V_WIDTH), BF16),
        jax.ShapeDtypeStruct((B, S, LANES), F32),
        jax.ShapeDtypeStruct((B, SUBLANES, S), F32),
        jax.ShapeDtypeStruct((B, S, 2 * D), BF16),
        jax.ShapeDtypeStruct((B, A_HEADS, S, A_QK_PAD), BF16),
        jax.ShapeDtypeStruct((B, A_HEADS, S, A_QK_PAD), BF16),
        jax.ShapeDtypeStruct((B, A_HEADS, S, V_HEAD), BF16),
    )
    out_specs = (
        tok(M_QK_WIDTH), tok(M_QK_WIDTH), tok(M_V_WIDTH), tok(M_V_WIDTH), tok(LANES),
        pl.BlockSpec((1, SUBLANES, tm), lambda b, s: (b, 0, s)),
        tok(2 * D), head(A_QK_PAD), head(A_QK_PAD), head(V_HEAD),
    )
    in_specs = [
        tok(D),
        pl.BlockSpec((1, N_MOD, D), lambda b, s: (b, 0, 0)),
        _const_spec((1, D)),
        _const_spec(wmain.shape), _const_spec(wifc.shape), _const_spec(wift.shape),
        _const_spec(conv_w.shape), _const_spec(conv_b.shape),
        _const_spec(qng.shape), _const_spec(kvng.shape),
        _const_spec(wuq.shape), _const_spec(wkn.shape), _const_spec(wv.shape),
        pl.BlockSpec((tm, 2 * LANES), lambda b, s: (s, 0)),
    ]
    return pl.pallas_call(
        _in_proj_kernel,
        grid=(B, ns),
        in_specs=in_specs,
        out_specs=out_specs,
        out_shape=out_shape,
        scratch_shapes=[pltpu.VMEM((SUBLANES, 2 * M_QK_WIDTH), F32)],
        compiler_params=pltpu.CompilerParams(
            dimension_semantics=("arbitrary", "arbitrary"), vmem_limit_bytes=VMEM_LIMIT),
        name="in_proj",
    )(x, mod, g, wmain, wifc, wift, conv_w, conv_b, qng, kvng, wuq, wkn, wv, cs)


def _log_sigmoid(x):
    return jnp.minimum(x, 0.0) - jnp.log(1.0 + jnp.exp(-jnp.abs(x)))


def _mlstm_kernel(q_ref, k_ref, v_ref, gc_ref, gr_ref, bc_ref, br_ref, y_ref, c_sc, n_sc, m_sc):
    L = q_ref.shape[1]

    @pl.when(pl.program_id(1) == 0)
    def _():
        c_sc[...] = jnp.zeros_like(c_sc)
        n_sc[...] = jnp.zeros_like(n_sc)
        m_sc[...] = jnp.zeros_like(m_sc)

    rows = lax.broadcasted_iota(I32, (L, L), 0)
    cols = lax.broadcasted_iota(I32, (L, L), 1)
    causal = cols <= rows
    tri = jnp.where(causal, 1.0, 0.0).astype(BF16)
    tri_t = jnp.where(rows <= cols, 1.0, 0.0).astype(BF16)

    gcol = gc_ref[0] + bc_ref[...]
    grow = gr_ref[0] + br_ref[:, :1]
    lf_col = _log_sigmoid(gcol)
    lf_row = _log_sigmoid(grow)
    ch, cl = _split_bf16(lf_col)
    b_col = _dot(tri, ch) + _dot(tri, cl)
    rh, rl = _split_bf16(lf_row)
    b_row = _dot(rh, tri_t) + _dot(rl, tri_t)

    for hh in range(M_HEADS):
        q = q_ref[0, :, hh * M_QK_DIM:(hh + 1) * M_QK_DIM]
        k = k_ref[0, :, hh * M_QK_DIM:(hh + 1) * M_QK_DIM]
        v = v_ref[0, :, hh * M_V_DIM:(hh + 1) * M_V_DIM]
        li_c = gcol[:, hh:hh + 1]
        b_c = b_col[:, M_HEADS + hh:M_HEADS + hh + 1]
        li_r = grow[hh:hh + 1, :]
        b_r = b_row[M_HEADS + hh:M_HEADS + hh + 1, :]
        m_prev = m_sc[hh][:, :1]
        c_prev = c_sc[hh]
        n_prev = n_sc[hh]

        dmat = jnp.where(causal, b_c - b_r + li_r, -jnp.inf)
        m_inter = b_c + m_prev
        m_t = jnp.maximum(jnp.max(dmat, axis=-1, keepdims=True), m_inter)
        s = _dot_nt(q, k) * jnp.exp(dmat - m_t)
        w_inter = jnp.exp(m_inter - m_t)
        num = _dot(s.astype(BF16), v) + w_inter * _dot(q, c_prev.astype(BF16))
        qn = jnp.sum(q.astype(F32) * n_prev, axis=-1, keepdims=True)
        den = jnp.sum(s, axis=-1, keepdims=True) + w_inter * qn
        hout = num / jnp.maximum(jnp.abs(den), jnp.exp(-m_t))
        y_ref[0, :, hh * M_V_DIM:(hh + 1) * M_V_DIM] = hout.astype(y_ref.dtype)

        b_last = b_c[L - 1:L, :]
        a_c = b_last - b_c + li_c
        m_new = jnp.maximum(b_last + m_prev, jnp.max(a_c, axis=0, keepdims=True))
        decay = jnp.exp(b_last + m_prev - m_new)
        wk = jnp.exp(a_c - m_new)
        wv = (wk * v.astype(F32)).astype(BF16)
        c_sc[hh] = decay * c_prev + _dot_tn(k, wv)
        n_sc[hh] = decay * n_prev + jnp.sum(wk * k.astype(F32), axis=0, keepdims=True)
        m_sc[hh] = jnp.broadcast_to(m_new, (1, LANES))


def _mlstm(q, k, v, gc, gr, bias_c, bias_r):
    B, S, _ = q.shape
    L = MLSTM_CHUNK
    tok = lambda c: pl.BlockSpec((1, L, c), lambda b, s: (b, s, 0))
    return pl.pallas_call(
        _mlstm_kernel,
        grid=(B, S // L),
        in_specs=[
            tok(M_QK_WIDTH), tok(M_QK_WIDTH), tok(M_V_WIDTH), tok(LANES),
            pl.BlockSpec((1, SUBLANES, L), lambda b, s: (b, 0, s)),
            _const_spec(bias_c.shape), _const_spec(bias_r.shape),
        ],
        out_specs=tok(M_V_WIDTH),
        out_shape=jax.ShapeDtypeStruct((B, S, M_V_WIDTH), BF16),
        scratch_shapes=[
            pltpu.VMEM((M_HEADS, M_QK_DIM, M_V_DIM), F32),
            pltpu.VMEM((M_HEADS, 1, M_QK_DIM), F32),
            pltpu.VMEM((M_HEADS, 1, LANES), F32),
        ],
        compiler_params=pltpu.CompilerParams(
            dimension_semantics=("arbitrary", "arbitrary"), vmem_limit_bytes=VMEM_LIMIT),
        name="mlstm",
    )(q, k, v, gc, gr, bias_c, bias_r)


def _attn_kernel(q_ref, k_ref, v_ref, o_ref, m_sc, acc_sc, s0_sc, s1_sc):
    nh = q_ref.shape[1]
    tq = q_ref.shape[2]
    qi = pl.program_id(2)
    nblk = TK // LANES

    m_sc[...] = jnp.full_like(m_sc, -jnp.inf)
    acc_sc[...] = jnp.zeros_like(acc_sc)
    ones_col = jnp.where(lax.broadcasted_iota(I32, (TK, LANES), 1) == 0, 1.0, 0.0).astype(BF16)

    def scores(hh, tile):
        k = k_ref[0, hh, pl.ds(pl.multiple_of(tile * TK, TK), TK), :]
        return _dot_nt(q_ref[0, hh], k)

    def consume(hh, s, tile, mask):
        v = v_ref[0, hh, pl.ds(pl.multiple_of(tile * TK, TK), TK), :]
        v = jnp.concatenate([v, ones_col], axis=1)
        if mask is not None:
            s = jnp.where(mask, s, -jnp.inf)
        blocks = [s[:, c * LANES:(c + 1) * LANES] for c in range(nblk)]
        mx = blocks[0]
        for blk in blocks[1:]:
            mx = jnp.maximum(mx, blk)
        m_prev = m_sc[hh]
        m_cur = jnp.maximum(m_prev, jnp.max(mx, axis=-1, keepdims=True))
        alpha = jnp.exp2(m_prev - m_cur)
        p = jnp.concatenate([jnp.exp2((blk - m_cur).astype(BF16)) for blk in blocks], axis=1)
        acc_sc[hh] = jnp.concatenate([alpha, alpha], axis=1) * acc_sc[hh] + _dot(p, v)
        m_sc[hh] = m_cur

    def step(src_sc, dst_sc, tile):
        for hh in range(nh):
            dst_sc[hh] = scores(hh, tile + 1)
            consume(hh, src_sc[hh], tile, None)

    def diagonal(src_sc):
        qc = lax.broadcasted_iota(I32, (tq, TK), 0) // CHUNK
        kc = lax.broadcasted_iota(I32, (tq, TK), 1) // CHUNK
        for hh in range(nh):
            consume(hh, src_sc[hh], qi, kc <= qc)

    for hh in range(nh):
        s0_sc[hh] = scores(hh, 0)

    def body(j, carry):
        step(s0_sc, s1_sc, 2 * j)
        step(s1_sc, s0_sc, 2 * j + 1)
        return carry

    lax.fori_loop(0, qi // 2, body, 0)

    @pl.when(qi % 2 == 0)
    def _():
        diagonal(s0_sc)

    @pl.when(qi % 2 == 1)
    def _():
        step(s0_sc, s1_sc, qi - 1)
        diagonal(s1_sc)

    for hh in range(nh):
        acc = acc_sc[hh]
        out = acc[:, :V_HEAD] / acc[:, V_HEAD:V_HEAD + 1]
        o_ref[0, :, hh * V_HEAD:(hh + 1) * V_HEAD] = out.astype(o_ref.dtype)


def _attention(qa, ka, va):
    B, H, S, _ = qa.shape
    nh = ATTN_HEADS_PER_STEP
    return pl.pallas_call(
        _attn_kernel,
        grid=(B, H // nh, S // TQ),
        in_specs=[
            pl.BlockSpec((1, nh, TQ, A_QK_PAD), lambda b, h, i: (b, h, i, 0)),
            pl.BlockSpec((1, nh, S, A_QK_PAD), lambda b, h, i: (b, h, 0, 0),
                         pipeline_mode=pl.Buffered(1)),
            pl.BlockSpec((1, nh, S, V_HEAD), lambda b, h, i: (b, h, 0, 0),
                         pipeline_mode=pl.Buffered(1)),
        ],
        out_specs=pl.BlockSpec((1, TQ, nh * V_HEAD), lambda b, h, i: (b, i, h)),
        out_shape=jax.ShapeDtypeStruct((B, S, H * V_HEAD), BF16),
        scratch_shapes=[
            pltpu.VMEM((nh, TQ, LANES), F32), pltpu.VMEM((nh, TQ, V_HEAD + LANES), F32),
            pltpu.VMEM((nh, TQ, TK), F32), pltpu.VMEM((nh, TQ, TK), F32),
        ],
        compiler_params=pltpu.CompilerParams(
            dimension_semantics=("arbitrary", "arbitrary", "arbitrary"),
            vmem_limit_bytes=VMEM_LIMIT),
        name="mla_attention",
    )(qa, ka, va)


_ROUTE_LANE0 = N_GROUPS


def _merge_kernel(x_ref, ym_ref, o_ref, ya_ref, gt_ref, mod_ref, g2_ref, wm_ref, wa_ref, wo_ref,
                  wrh_ref, wrl_ref, br_ref,
                  xo_ref, h2_ref, ri_ref, cnt_ref, carry_sc):
    tm = x_ref.shape[1]
    D = x_ref.shape[2]

    @pl.when((pl.program_id(0) == 0) & (pl.program_id(1) == 0))
    def _():
        carry_sc[...] = jnp.zeros_like(carry_sc)

    ym = ym_ref[0].astype(F32) * _sigmoid(o_ref[0].astype(F32))
    bm = _dot(ym.astype(BF16), wm_ref[...])
    ba = _dot(ya_ref[0], wa_ref[...])
    gt = gt_ref[0].astype(F32)
    merged = _sigmoid(gt[:, :D]) * bm + _sigmoid(gt[:, D:]) * ba
    out = _dot(merged.astype(BF16), wo_ref[...])
    xn = x_ref[0] + mod_ref[0, 2:3, :] * out
    xo_ref[0] = xn

    h2 = _rms(xn, g2_ref[...]) * (1.0 + mod_ref[0, 4:5, :]) + mod_ref[0, 3:4, :]
    h2_ref[0, :, :D] = h2

    hh, hl = _split_bf16(h2)
    logits = _dot(hh, wrh_ref[...]) + _dot(hl, wrh_ref[...]) + _dot(hh, wrl_ref[...]) + br_ref[...]
    lane = lax.broadcasted_iota(I32, (tm, LANES), 1).astype(F32)
    big = float(LANES)
    is_grp = lane < N_GROUPS
    gl = jnp.where(is_grp, logits, -jnp.inf)
    gmax = jnp.max(gl, axis=-1, keepdims=True)
    grp = jnp.min(jnp.where(gl == gmax, lane, big), axis=-1, keepdims=True)
    p_grp = 1.0 / jnp.sum(jnp.where(is_grp, jnp.exp(logits - gmax), 0.0), axis=-1, keepdims=True)
    lo = _ROUTE_LANE0 + grp * EXPERTS_PER_GROUP
    el = jnp.where((lane >= lo) & (lane < lo + EXPERTS_PER_GROUP), logits, -jnp.inf)
    m1 = jnp.max(el, axis=-1, keepdims=True)
    i1 = jnp.min(jnp.where(el == m1, lane, big), axis=-1, keepdims=True)
    el2 = jnp.where(lane == i1, -jnp.inf, el)
    m2 = jnp.max(el2, axis=-1, keepdims=True)
    i2 = jnp.min(jnp.where(el2 == m2, lane, big), axis=-1, keepdims=True)
    r = jnp.exp(m2 - m1)
    w1 = p_grp / (1.0 + r)
    w2 = p_grp * r / (1.0 + r)

    j1 = i1 - lo
    j2 = i2 - lo
    ja = jnp.minimum(j1, j2)
    jb = jnp.maximum(j1, j2)
    wa = jnp.where(j1 < j2, w1, w2)
    wb = jnp.where(j1 < j2, w2, w1)
    pair = ja * (2 * EXPERTS_PER_GROUP - 1 - ja) * 0.5 + (jb - ja - 1.0)
    bucket = grp * PAIRS_PER_GROUP + pair
    hot = lane == bucket
    onehot = jnp.where(hot, 1.0, 0.0)
    rows = lax.broadcasted_iota(I32, (tm, tm), 0)
    cols = lax.broadcasted_iota(I32, (tm, tm), 1)
    strict = jnp.where(cols < rows, 1.0, 0.0).astype(BF16)
    before = _dot(strict, onehot.astype(BF16)) + carry_sc[...]
    rank = jnp.sum(jnp.where(hot, before, 0.0), axis=-1, keepdims=True)
    carry = carry_sc[...] + jnp.sum(onehot, axis=0, keepdims=True)
    carry_sc[...] = carry
    cnt_ref[...] = carry

    route = jnp.where(lane == 0, bucket, jnp.where(lane == 1, rank, 0.0))
    ri_ref[0] = route.T[:SUBLANES, :].astype(I32)
    h2_ref[0, :, D:] = jnp.where(lane == 0, wa, jnp.where(lane == 1, wb, 0.0))


def _merge(x, ym, o, ya, gt, mod, g2, wm, wa, wo, wrh, wrl, br):
    B, S, D = x.shape
    tm = TM_PROJ
    tok = lambda c: pl.BlockSpec((1, tm, c), lambda b, s: (b, s, 0))
    out_shape = (
        jax.ShapeDtypeStruct((B, S, D), F32),
        jax.ShapeDtypeStruct((B, S, D + LANES), F32),
        jax.ShapeDtypeStruct((B, SUBLANES, S), I32),
        jax.ShapeDtypeStruct((1, LANES), F32),
    )
    out_specs = (tok(D), tok(D + LANES),
                 pl.BlockSpec((1, SUBLANES, tm), lambda b, s: (b, 0, s)),
                 pl.BlockSpec((1, LANES), lambda b, s: (0, 0)))
    in_specs = [
        tok(D), tok(M_V_WIDTH), tok(M_V_WIDTH), tok(A_HEADS * V_HEAD), tok(2 * D),
        pl.BlockSpec((1, N_MOD, D), lambda b, s: (b, 0, 0)),
        _const_spec((1, D)),
        _const_spec(wm.shape), _const_spec(wa.shape), _const_spec(wo.shape),
        _const_spec(wrh.shape), _const_spec(wrl.shape), _const_spec(br.shape),
    ]
    return pl.pallas_call(
        _merge_kernel,
        grid=(B, S // tm),
        in_specs=in_specs,
        out_specs=out_specs,
        out_shape=out_shape,
        scratch_shapes=[pltpu.VMEM((1, LANES), F32)],
        compiler_params=pltpu.CompilerParams(
            dimension_semantics=("arbitrary", "arbitrary"), vmem_limit_bytes=VMEM_LIMIT),
        name="merge_router",
    )(x, ym, o, ya, gt, mod, g2, wm, wa, wo, wrh, wrl, br)


def _row_copy(src, si, dst, di, sem):
    return pltpu.make_async_copy(src.at[pl.ds(si, 1), :], dst.at[pl.ds(di, 1), :], sem)


def _tail_copy(zero_sc, xg_ref, row, tsem):
    return pltpu.make_async_copy(zero_sc, xg_ref.at[pl.ds(row, MOE_ROWS), :], tsem)


def _dispatch_kernel(bstart_ref, padlo_ref, bend_ref, bkt_ref, rank_ref, h_ref, xg_ref, dest_ref,
                     zero_sc, sem, zsem, tsem):
    n = bkt_ref.shape[0]
    step = pl.program_id(0)
    used = bend_ref[N_BUCKETS - 1]
    n_tail = (xg_ref.shape[0] - used) // MOE_ROWS

    @pl.when(step == 0)
    def _():
        zero_sc[...] = jnp.zeros_like(zero_sc)

        def per_bucket(b, c):
            def one(r, cc):
                _row_copy(zero_sc, 0, xg_ref, r, zsem).start()
                return cc
            lax.fori_loop(padlo_ref[b], bend_ref[b], one, 0)
            return c
        lax.fori_loop(0, N_BUCKETS, per_bucket, 0)

        def tail(j, c):
            _tail_copy(zero_sc, xg_ref, pl.multiple_of(used + j * MOE_ROWS, MOE_ROWS), tsem).start()
            return c
        lax.fori_loop(0, n_tail, tail, 0)

    def issue(a, c):
        dst = bstart_ref[bkt_ref[a]] + rank_ref[a]
        dest_ref[a] = dst
        _row_copy(h_ref, a, xg_ref, dst, sem).start()
        return c
    lax.fori_loop(0, n, issue, 0, unroll=4)

    def drain(a, c):
        _row_copy(h_ref, 0, xg_ref, 0, sem).wait()
        return c
    lax.fori_loop(0, n, drain, 0, unroll=8)

    @pl.when(step == pl.num_programs(0) - 1)
    def _():
        n_fill = used - pl.num_programs(0) * n

        def zdrain(a, c):
            _row_copy(zero_sc, 0, xg_ref, 0, zsem).wait()
            return c
        lax.fori_loop(0, n_fill, zdrain, 0)

        def tdrain(j, c):
            _tail_copy(zero_sc, xg_ref, 0, tsem).wait()
            return c
        lax.fori_loop(0, n_tail, tdrain, 0)


def _dispatch(bstart, padlo, bend, bkt, rank, h2e, cap):
    T, W = h2e.shape
    n = DISPATCH_ROWS
    smem_blk = pl.BlockSpec((n,), lambda i, *_: (i,), memory_space=pltpu.SMEM)
    grid_spec = pltpu.PrefetchScalarGridSpec(
        num_scalar_prefetch=3,
        grid=(T // n,),
        in_specs=[smem_blk, smem_blk, pl.BlockSpec((n, W), lambda i, *_: (i, 0))],
        out_specs=[pl.BlockSpec(memory_space=pl.ANY), smem_blk],
        scratch_shapes=[pltpu.VMEM((MOE_ROWS, W), F32), pltpu.SemaphoreType.DMA,
                        pltpu.SemaphoreType.DMA, pltpu.SemaphoreType.DMA],
    )
    return pl.pallas_call(
        _dispatch_kernel,
        grid_spec=grid_spec,
        out_shape=[jax.ShapeDtypeStruct((cap, W), F32),
                   jax.ShapeDtypeStruct((T,), I32)],
        compiler_params=pltpu.CompilerParams(
            dimension_semantics=("arbitrary",), vmem_limit_bytes=VMEM_LIMIT),
        name="moe_dispatch",
    )(bstart, padlo, bend, bkt, rank, h2e)


def _expert_kernel(blk_lo_ref, blk_hi_ref, nused_ref, x_ref, wgu_lo_ref, wd_lo_ref, wgu_hi_ref,
                   wd_hi_ref, o_ref):
    i = pl.program_id(0)
    D = o_ref.shape[1]

    @pl.when(i < nused_ref[0])
    def _():
        xb = x_ref[:, :D].astype(BF16)
        w = x_ref[:, D:]

        def ffn(wgu_ref, wd_ref):
            gu = _dot(xb, wgu_ref[0])
            a = _silu(gu[:, :D_EXPERT]) * gu[:, D_EXPERT:]
            return _dot(a.astype(BF16), wd_ref[0])

        o_ref[...] = w[:, 0:1] * ffn(wgu_lo_ref, wd_lo_ref) + w[:, 1:2] * ffn(wgu_hi_ref, wd_hi_ref)

    @pl.when(i >= nused_ref[0])
    def _():
        o_ref[...] = jnp.zeros_like(o_ref)


def _experts(blk_lo, blk_hi, nused, xg, wgu, wd):
    cap, W = xg.shape
    D = W - LANES
    R = MOE_ROWS
    grid_spec = pltpu.PrefetchScalarGridSpec(
        num_scalar_prefetch=3,
        grid=(cap // R,),
        in_specs=[
            pl.BlockSpec((R, W), lambda i, lo, hi, nu: (jnp.minimum(i, nu[0] - 1), 0)),
            pl.BlockSpec((1, D, 2 * D_EXPERT), lambda i, lo, hi, nu: (lo[i], 0, 0)),
            pl.BlockSpec((1, D_EXPERT, D), lambda i, lo, hi, nu: (lo[i], 0, 0)),
            pl.BlockSpec((1, D, 2 * D_EXPERT), lambda i, lo, hi, nu: (hi[i], 0, 0)),
            pl.BlockSpec((1, D_EXPERT, D), lambda i, lo, hi, nu: (hi[i], 0, 0)),
        ],
        out_specs=pl.BlockSpec((R, D), lambda i, lo, hi, nu: (i, 0)),
    )
    return pl.pallas_call(
        _expert_kernel,
        grid_spec=grid_spec,
        out_shape=jax.ShapeDtypeStruct((cap, D), F32),
        compiler_params=pltpu.CompilerParams(
            dimension_semantics=("arbitrary",), vmem_limit_bytes=VMEM_LIMIT),
        name="moe_experts",
    )(blk_lo, blk_hi, nused, xg, wgu, wd, wgu, wd)


def _combine_kernel(dest_ref, og_ref, x_ref, mod_ref, fg_ref, o_ref, buf_sc, sem, *, final_norm):
    tb = x_ref.shape[1]

    def issue(t, c):
        _row_copy(og_ref, dest_ref[t], buf_sc, t, sem).start()
        return c
    lax.fori_loop(0, tb, issue, 0, unroll=4)

    def drain(t, c):
        _row_copy(og_ref, 0, buf_sc, 0, sem).wait()
        return c
    lax.fori_loop(0, tb, drain, 0, unroll=8)

    xo = x_ref[0] + mod_ref[0, 5:6, :] * buf_sc[...]
    if final_norm:
        xo = _rms(xo, fg_ref[...])
    o_ref[0] = xo


def _combine(dest, outg, x, mod, fg, final_norm):
    B, S, D = x.shape
    tb = TB_COMBINE
    ns = S // tb
    return pl.pallas_call(
        functools.partial(_combine_kernel, final_norm=final_norm),
        grid=(B, ns),
        in_specs=[
            pl.BlockSpec((tb,), lambda b, s: (b * ns + s,), memory_space=pltpu.SMEM),
            pl.BlockSpec(memory_space=pl.ANY),
            pl.BlockSpec((1, tb, D), lambda b, s: (b, s, 0)),
            pl.BlockSpec((1, N_MOD, D), lambda b, s: (b, 0, 0)),
            pl.BlockSpec((1, D), lambda b, s: (0, 0)),
        ],
        out_specs=pl.BlockSpec((1, tb, D), lambda b, s: (b, s, 0)),
        out_shape=jax.ShapeDtypeStruct((B, S, D), F32),
        scratch_shapes=[pltpu.VMEM((tb, D), F32), pltpu.SemaphoreType.DMA],
        compiler_params=pltpu.CompilerParams(
            dimension_semantics=("arbitrary", "arbitrary"), vmem_limit_bytes=VMEM_LIMIT),
        name="moe_combine",
    )(dest, outg, x, mod, fg)


def _rotate_half_cols(w):
    half = w.shape[-1] // 2
    return jnp.concatenate([-w[..., half:], w[..., :half]], axis=-1)


def _pack_layer_weights(l, w_in, w_uq, w_ukv, w_group, w_router, b_group, b_router,
                        w_expert_gate, w_expert_up, igate_b, fgate_b):
    D = w_in.shape[1]
    sizes = (M_QK_WIDTH, M_QK_WIDTH, M_V_WIDTH, M_V_WIDTH, M_HEADS, M_HEADS, Q_LORA, KV_LORA,
             QK_ROPE, 2 * D)
    pts = [sum(sizes[:i + 1]) for i in range(len(sizes) - 1)]
    wq, wk, wv, wo, wi, wf, wcq, wckv, wkr, wg = jnp.split(w_in[l], pts, axis=1)
    wmain = jnp.concatenate([wq, wk, wv, wo, wkr, _rotate_half_cols(wkr), wcq, wckv, wg],
                            axis=1).astype(BF16)
    wif = jnp.concatenate([wi, wf], axis=1)
    wifc = jnp.pad(wif, ((0, 0), (0, LANES - 2 * M_HEADS))).astype(BF16)
    wift = wif.T.astype(BF16)

    uq = w_uq[l].reshape(Q_LORA, A_HEADS, QK_NOPE + QK_ROPE)
    uq_rope = uq[:, :, QK_NOPE:]
    wuq = jnp.concatenate([uq[:, :, :QK_NOPE], uq_rope, _rotate_half_cols(uq_rope)], axis=-1)
    wuq = wuq.reshape(Q_LORA, A_HEADS * A_QK_PAD).astype(BF16)
    ukv = w_ukv[l].reshape(KV_LORA, A_HEADS, QK_NOPE + V_HEAD)
    wkn = ukv[:, :, :QK_NOPE].reshape(KV_LORA, A_HEADS * QK_NOPE).astype(BF16)
    wvv = ukv[:, :, QK_NOPE:].reshape(KV_LORA, A_HEADS * V_HEAD).astype(BF16)

    wr = jnp.concatenate([w_group[l], w_router[l]], axis=1)
    wr = jnp.pad(wr, ((0, 0), (0, LANES - wr.shape[1])))
    wrh = wr.astype(BF16)
    wrl = (wr - wrh.astype(F32)).astype(BF16)
    br = jnp.concatenate([b_group[l], b_router[l]])
    br = jnp.pad(br, (0, LANES - br.shape[0])).reshape(1, LANES)

    wgu = jnp.concatenate([w_expert_gate[l], w_expert_up[l]], axis=-1).astype(BF16)

    gb = jnp.concatenate([igate_b[l], fgate_b[l]])
    bias_c = jnp.pad(gb, (0, LANES - gb.shape[0])).reshape(1, LANES)
    bias_r = jnp.broadcast_to(gb[:, None], (2 * M_HEADS, LANES))
    return wmain, wifc, wift, wuq, wkn, wvv, wrh, wrl, br, wgu, bias_c, bias_r


def kernel(x, c, mod_w, mod_b, norm1_g, w_in, conv_w, conv_b, igate_b, fgate_b, q_norm_g,
           kv_norm_g, w_uq, w_ukv, w_branch_m, w_branch_a, w_out, norm2_g, w_group, b_group,
           w_router, b_router, w_expert_gate, w_expert_up, w_expert_down, final_norm_g):
    B, S, D = x.shape
    L = mod_w.shape[0]
    T = B * S
    assert S % TQ == 0 and S % MLSTM_CHUNK == 0 and S % TM_PROJ == 0 and S % TB_COMBINE == 0
    assert TQ == TK and TQ % CHUNK == 0 and T % DISPATCH_ROWS == 0
    assert A_HEADS % ATTN_HEADS_PER_STEP == 0 and V_HEAD == LANES
    assert N_BUCKETS <= LANES and TOP_K == 2

    pos = jnp.arange(S, dtype=F32)
    inv_freq = 1.0 / (ROPE_THETA ** (jnp.arange(0, QK_ROPE, 2, dtype=F32) / QK_ROPE))
    ang = pos[:, None] * inv_freq[None, :]
    cos = jnp.cos(ang)
    sin = jnp.sin(ang)
    z = jnp.zeros((S, QK_ROPE), F32)
    cs = jnp.concatenate([cos, cos, z, sin, sin, z], axis=1)

    mod_all = _modulation(c, mod_w, mod_b).reshape(L, B, N_MOD, D)

    cap = T + N_BUCKETS * MOE_ROWS
    n_blocks = cap // MOE_ROWS
    pairs = [(a, b) for a in range(EXPERTS_PER_GROUP) for b in range(a + 1, EXPERTS_PER_GROUP)]
    bucket_lo = jnp.array([g * EXPERTS_PER_GROUP + a for g in range(N_GROUPS) for a, _ in pairs], I32)
    bucket_hi = jnp.array([g * EXPERTS_PER_GROUP + b for g in range(N_GROUPS) for _, b in pairs], I32)

    for l in range(L):
        (wmain, wifc, wift, wuq, wkn, wvv, wrh, wrl, br, wgu, bias_c, bias_r) = _pack_layer_weights(
            l, w_in, w_uq, w_ukv, w_group, w_router, b_group, b_router, w_expert_gate,
            w_expert_up, igate_b, fgate_b)
        mod = mod_all[l]

        q, k, v, o, gc, gr, gt, qa, ka, va = _in_proj(
            x, mod, norm1_g[l].reshape(1, D), wmain, wifc, wift, conv_w[l],
            conv_b[l].reshape(1, -1), q_norm_g[l].reshape(1, -1), kv_norm_g[l].reshape(1, -1),
            wuq, wkn, wvv, cs)
        ym = _mlstm(q, k, v, gc, gr, bias_c, bias_r)
        ya = _attention(qa, ka, va)
        xn, h2e, ri, cnt = _merge(
            x, ym, o, ya, gt, mod, norm2_g[l].reshape(1, D), w_branch_m[l].astype(BF16),
            w_branch_a[l].astype(BF16), w_out[l].astype(BF16), wrh, wrl, br)

        counts = cnt[0, :N_BUCKETS].astype(I32)
        padded = (counts + MOE_ROWS - 1) // MOE_ROWS * MOE_ROWS
        bend = jnp.cumsum(padded).astype(I32)
        bstart = bend - padded
        padlo = bstart + counts
        blk_start = jnp.arange(n_blocks, dtype=I32) * MOE_ROWS
        blk_b = jnp.minimum(jnp.sum(bend[None, :] <= blk_start[:, None], axis=1), N_BUCKETS - 1)
        nused = (bend[N_BUCKETS - 1:] // MOE_ROWS).astype(I32)
        bkt = ri[:, 0, :].reshape(T)
        rank = ri[:, 1, :].reshape(T)

        xg, dest = _dispatch(bstart, padlo, bend, bkt, rank, h2e.reshape(T, D + LANES), cap)
        outg = _experts(bucket_lo[blk_b], bucket_hi[blk_b], nused, xg, wgu,
                        w_expert_down[l].astype(BF16))
        x = _combine(dest, outg, xn, mod, final_norm_g.reshape(1, D), final_norm=(l == L - 1))
    return x
```

```python
import functools
import math

import jax
import jax.numpy as jnp
from jax import lax
from jax.experimental import pallas as pl
from jax.experimental.pallas import tpu as pltpu

F32 = jnp.float32
BF16 = jnp.bfloat16
I32 = jnp.int32

EPS = 1e-6
CHUNK = 64
N_MOD = 6

M_HEADS = 4
M_QK_DIM = 128
M_V_DIM = 256
M_QK_WIDTH = M_HEADS * M_QK_DIM
M_V_WIDTH = M_HEADS * M_V_DIM
CONV_WIDTH = 4

A_HEADS = 8
Q_LORA = 384
KV_LORA = 256
QK_NOPE = 128
QK_ROPE = 64
V_HEAD = 128
A_QK_PAD = 256
ROPE_THETA = 10000.0

N_GROUPS = 4
EXPERTS_PER_GROUP = 8
N_EXPERTS = N_GROUPS * EXPERTS_PER_GROUP
PAIRS_PER_GROUP = EXPERTS_PER_GROUP * (EXPERTS_PER_GROUP - 1) // 2
N_BUCKETS = N_GROUPS * PAIRS_PER_GROUP
TOP_K = 2
D_EXPERT = 256

LANES = 128
SUBLANES = 8

TM_PROJ = 512
MLSTM_CHUNK = 256
TQ = 512
TK = 512
ATTN_HEADS_PER_STEP = 4
MOE_ROWS = 256
DISPATCH_ROWS = 2048
TB_COMBINE = 512

VMEM_LIMIT = 56 * 1024 * 1024

_OFF_QK = 0
_OFF_V = _OFF_QK + 2 * M_QK_WIDTH
_OFF_O = _OFF_V + M_V_WIDTH
_OFF_KR = _OFF_O + M_V_WIDTH
_OFF_CQ = _OFF_KR + 2 * QK_ROPE
_OFF_CKV = _OFF_CQ + Q_LORA
_OFF_G = _OFF_CKV + KV_LORA
_N_MAIN = _OFF_G + 2 * 1024


def _dot(a, b):
    return jnp.dot(a, b, preferred_element_type=F32)


def _dot_nt(a, b):
    return lax.dot_general(a, b, (((1,), (1,)), ((), ())), preferred_element_type=F32)


def _dot_tn(a, b):
    return lax.dot_general(a, b, (((0,), (0,)), ((), ())), preferred_element_type=F32)


def _split_bf16(x):
    hi = x.astype(BF16)
    lo = (x - hi.astype(F32)).astype(BF16)
    return hi, lo


def _sigmoid(x):
    return 1.0 / (1.0 + jnp.exp(-x))


def _silu(x):
    return x * _sigmoid(x)


def _rms(x, g):
    return x * lax.rsqrt(jnp.mean(x * x, axis=-1, keepdims=True) + EPS) * g


def _const_spec(shape):
    nd = len(shape)
    return pl.BlockSpec(shape, lambda *_: (0,) * nd, pipeline_mode=pl.Buffered(1))


def _mod_kernel(c_ref, w_ref, b_ref, o_ref):
    c = c_ref[...]
    cond = _silu(c)
    ch, cl = _split_bf16(cond)
    wh, wl = _split_bf16(w_ref[0])
    o_ref[0] = _dot(ch, wh) + _dot(cl, wh) + _dot(ch, wl) + b_ref[0]


def _modulation(c, mod_w, mod_b):
    L, D, N = mod_w.shape
    B = c.shape[0]
    tn = 1536
    return pl.pallas_call(
        _mod_kernel,
        grid=(L, N // tn),
        in_specs=[
            pl.BlockSpec((B, D), lambda l, j: (0, 0)),
            pl.BlockSpec((1, D, tn), lambda l, j: (l, 0, j)),
            pl.BlockSpec((1, 1, tn), lambda l, j: (l, 0, j)),
        ],
        out_specs=pl.BlockSpec((1, B, tn), lambda l, j: (l, 0, j)),
        out_shape=jax.ShapeDtypeStruct((L, B, N), F32),
        compiler_params=pltpu.CompilerParams(
            dimension_semantics=("arbitrary", "arbitrary"), vmem_limit_bytes=VMEM_LIMIT),
        name="adaln_mod",
    )(c, mod_w, mod_b.reshape(L, 1, N))


def _rope128(blk, cs):
    return blk * cs[:, :LANES] + pltpu.roll(blk, QK_ROPE, 1) * cs[:, LANES:]


def _in_proj_kernel(x_ref, mod_ref, g_ref, w_ref, wifc_ref, wift_ref, cw_ref, cb_ref, qng_ref,
                    kvng_ref, wuq_ref, wkn_ref, wv_ref, cs_ref, bc_ref, br_ref,
                    ym_ref, o_ref, gt_ref, qa_ref, ka_ref, va_ref,
                    tail_sc, c_sc, n_sc, m_sc):
    tm = x_ref.shape[1]

    @pl.when(pl.program_id(1) == 0)
    def _():
        tail_sc[...] = jnp.zeros_like(tail_sc)
        c_sc[...] = jnp.zeros_like(c_sc)
        n_sc[...] = jnp.zeros_like(n_sc)
        m_sc[...] = jnp.zeros_like(m_sc)

    x = x_ref[0]
    shift = mod_ref[0, 0:1, :]
    scale = mod_ref[0, 1:2, :]
    h = _rms(x, g_ref[...]) * (1.0 + scale) + shift
    hb = h.astype(BF16)

    raw = _dot(hb, w_ref[:, _OFF_QK:_OFF_V])
    tail = tail_sc[...]
    row8 = lax.broadcasted_iota(I32, (SUBLANES, 1), 0)
    acc = cb_ref[...] + cw_ref[CONV_WIDTH - 1:CONV_WIDTH, :] * raw
    for j in range(1, CONV_WIDTH):
        sh = pltpu.roll(raw, j, 0)
        fix = pltpu.roll(tail, j, 0)
        first = jnp.where(row8 < j, fix, sh[:SUBLANES])
        sh = jnp.concatenate([first, sh[SUBLANES:]], axis=0)
        acc = acc + cw_ref[CONV_WIDTH - 1 - j:CONV_WIDTH - j, :] * sh
    tail_sc[...] = raw[tm - SUBLANES:, :]
    qk = _silu(acc)
    q_m = (qk[:, :M_QK_WIDTH] * (M_QK_DIM ** -0.5)).astype(BF16)
    k_m = qk[:, M_QK_WIDTH:].astype(BF16)
    v_m = _dot(hb, w_ref[:, _OFF_V:_OFF_O]).astype(BF16)
    gcol = _dot(hb, wifc_ref[...]) + bc_ref[...]
    grow = _dot_nt(wift_ref[...], hb) + br_ref[:, :1]
    for c in range(tm // MLSTM_CHUNK):
        r0, r1 = c * MLSTM_CHUNK, (c + 1) * MLSTM_CHUNK
        _mlstm_chunk(q_m[r0:r1], k_m[r0:r1], v_m[r0:r1], gcol[r0:r1], grow[:, r0:r1],
                     ym_ref, r0, c_sc, n_sc, m_sc)

    o_ref[0] = _dot(hb, w_ref[:, _OFF_O:_OFF_KR]).astype(BF16)
    gt_ref[0] = _dot(hb, w_ref[:, _OFF_G:_N_MAIN]).astype(BF16)

    cs = cs_ref[...]
    kr = _rope128(_dot(hb, w_ref[:, _OFF_KR:_OFF_CQ]), cs).astype(BF16)

    cq = _dot(hb, w_ref[:, _OFF_CQ:_OFF_CKV])
    cqn = _rms(cq, qng_ref[...]).astype(BF16)
    qa = _dot(cqn, wuq_ref[...])
    att_scale = (QK_NOPE + QK_ROPE) ** -0.5 * math.log2(math.e)
    for hh in range(A_HEADS):
        base = hh * A_QK_PAD
        qa_ref[0, hh, :, :QK_NOPE] = (qa[:, base:base + QK_NOPE] * att_scale).astype(BF16)
        roped = _rope128(qa[:, base + QK_NOPE:base + A_QK_PAD], cs)
        qa_ref[0, hh, :, QK_NOPE:] = (roped * att_scale).astype(BF16)

    ckv = _dot(hb, w_ref[:, _OFF_CKV:_OFF_G])
    ckvn = _rms(ckv, kvng_ref[...]).astype(BF16)
    kn = _dot(ckvn, wkn_ref[...]).astype(BF16)
    vv = _dot(ckvn, wv_ref[...]).astype(BF16)
    for hh in range(A_HEADS):
        ka_ref[0, hh, :, :QK_NOPE] = kn[:, hh * QK_NOPE:(hh + 1) * QK_NOPE]
        ka_ref[0, hh, :, QK_NOPE:] = kr
        va_ref[0, hh] = vv[:, hh * V_HEAD:(hh + 1) * V_HEAD]


def _in_proj(x, mod, g, wmain, wifc, wift, conv_w, conv_b, qng, kvng, wuq, wkn, wv, cs, bias_c,
             bias_r):
    B, S, D = x.shape
    tm = TM_PROJ
    ns = S // tm
    tok = lambda c: pl.BlockSpec((1, tm, c), lambda b, s: (b, s, 0))
    head = lambda c: pl.BlockSpec((1, A_HEADS, tm, c), lambda b, s: (b, 0, s, 0))
    out_shape = (
        jax.ShapeDtypeStruct((B, S, M_V_WIDTH), BF16),
        jax.ShapeDtypeStruct((B, S, M_V_WIDTH), BF16),
        jax.ShapeDtypeStruct((B, S, 2 * D), BF16),
        jax.ShapeDtypeStruct((B, A_HEADS, S, A_QK_PAD), BF16),
        jax.ShapeDtypeStruct((B, A_HEADS, S, A_QK_PAD), BF16),
        jax.ShapeDtypeStruct((B, A_HEADS, S, V_HEAD), BF16),
    )
    out_specs = (
        tok(M_V_WIDTH), tok(M_V_WIDTH), tok(2 * D), head(A_QK_PAD), head(A_QK_PAD), head(V_HEAD),
    )
    in_specs = [
        tok(D),
        pl.BlockSpec((1, N_MOD, D), lambda b, s: (b, 0, 0)),
        _const_spec((1, D)),
        _const_spec(wmain.shape), _const_spec(wifc.shape), _const_spec(wift.shape),
        _const_spec(conv_w.shape), _const_spec(conv_b.shape),
        _const_spec(qng.shape), _const_spec(kvng.shape),
        _const_spec(wuq.shape), _const_spec(wkn.shape), _const_spec(wv.shape),
        pl.BlockSpec((tm, 2 * LANES), lambda b, s: (s, 0)),
        _const_spec(bias_c.shape), _const_spec(bias_r.shape),
    ]
    return pl.pallas_call(
        _in_proj_kernel,
        grid=(B, ns),
        in_specs=in_specs,
        out_specs=out_specs,
        out_shape=out_shape,
        scratch_shapes=[
            pltpu.VMEM((SUBLANES, 2 * M_QK_WIDTH), F32),
            pltpu.VMEM((M_HEADS, M_QK_DIM, M_V_DIM), F32),
            pltpu.VMEM((M_HEADS, 1, M_QK_DIM), F32),
            pltpu.VMEM((M_HEADS, 1, LANES), F32),
        ],
        compiler_params=pltpu.CompilerParams(
            dimension_semantics=("arbitrary", "arbitrary"), vmem_limit_bytes=VMEM_LIMIT),
        name="in_proj",
    )(x, mod, g, wmain, wifc, wift, conv_w, conv_b, qng, kvng, wuq, wkn, wv, cs, bias_c, bias_r)


def _log_sigmoid(x):
    return jnp.minimum(x, 0.0) - jnp.log(1.0 + jnp.exp(-jnp.abs(x)))


def _mlstm_chunk(q_all, k_all, v_all, gcol, grow, y_ref, row0, c_sc, n_sc, m_sc):
    L = q_all.shape[0]
    rows = lax.broadcasted_iota(I32, (L, L), 0)
    cols = lax.broadcasted_iota(I32, (L, L), 1)
    causal = cols <= rows
    tri = jnp.where(causal, 1.0, 0.0).astype(BF16)
    tri_t = jnp.where(rows <= cols, 1.0, 0.0).astype(BF16)

    ch, cl = _split_bf16(_log_sigmoid(gcol))
    b_col = _dot(tri, ch) + _dot(tri, cl)
    rh, rl = _split_bf16(_log_sigmoid(grow))
    b_row = _dot(rh, tri_t) + _dot(rl, tri_t)

    for hh in range(M_HEADS):
        q = q_all[:, hh * M_QK_DIM:(hh + 1) * M_QK_DIM]
        k = k_all[:, hh * M_QK_DIM:(hh + 1) * M_QK_DIM]
        v = v_all[:, hh * M_V_DIM:(hh + 1) * M_V_DIM]
        li_c = gcol[:, hh:hh + 1]
        b_c = b_col[:, M_HEADS + hh:M_HEADS + hh + 1]
        li_r = grow[hh:hh + 1, :]
        b_r = b_row[M_HEADS + hh:M_HEADS + hh + 1, :]
        m_prev = m_sc[hh][:, :1]
        c_prev = c_sc[hh]
        n_prev = n_sc[hh]

        dmat = jnp.where(causal, b_c - b_r + li_r, -jnp.inf)
        m_inter = b_c + m_prev
        m_t = jnp.maximum(jnp.max(dmat, axis=-1, keepdims=True), m_inter)
        s = _dot_nt(q, k) * jnp.exp(dmat - m_t)
        w_inter = jnp.exp(m_inter - m_t)
        num = _dot(s.astype(BF16), v) + w_inter * _dot(q, c_prev.astype(BF16))
        qn = jnp.sum(q.astype(F32) * n_prev, axis=-1, keepdims=True)
        den = jnp.sum(s, axis=-1, keepdims=True) + w_inter * qn
        hout = num / jnp.maximum(jnp.abs(den), jnp.exp(-m_t))
        y_ref[0, row0:row0 + L, hh * M_V_DIM:(hh + 1) * M_V_DIM] = hout.astype(y_ref.dtype)

        b_last = b_c[L - 1:L, :]
        a_c = b_last - b_c + li_c
        m_new = jnp.maximum(b_last + m_prev, jnp.max(a_c, axis=0, keepdims=True))
        decay = jnp.exp(b_last + m_prev - m_new)
        wk = jnp.exp(a_c - m_new)
        wv = (wk * v.astype(F32)).astype(BF16)
        c_sc[hh] = decay * c_prev + _dot_tn(k, wv)
        n_sc[hh] = decay * n_prev + jnp.sum(wk * k.astype(F32), axis=0, keepdims=True)
        m_sc[hh] = jnp.broadcast_to(m_new, (1, LANES))


def _attn_kernel(q_ref, k_ref, v_ref, o_ref, m_sc, acc_sc, s0_sc, s1_sc):
    nh = q_ref.shape[1]
    tq = q_ref.shape[2]
    qi = pl.program_id(2)
    nblk = TK // LANES

    m_sc[...] = jnp.full_like(m_sc, -jnp.inf)
    acc_sc[...] = jnp.zeros_like(acc_sc)
    ones_col = jnp.where(lax.broadcasted_iota(I32, (TK, LANES), 1) == 0, 1.0, 0.0).astype(BF16)

    def scores(hh, tile):
        k = k_ref[0, hh, pl.ds(pl.multiple_of(tile * TK, TK), TK), :]
        return _dot_nt(q_ref[0, hh], k)

    def consume(hh, s, tile, mask):
        v = v_ref[0, hh, pl.ds(pl.multiple_of(tile * TK, TK), TK), :]
        v = jnp.concatenate([v, ones_col], axis=1)
        if mask is not None:
            s = jnp.where(mask, s, -jnp.inf)
        blocks = [s[:, c * LANES:(c + 1) * LANES] for c in range(nblk)]
        mx = blocks[0]
        for blk in blocks[1:]:
            mx = jnp.maximum(mx, blk)
        m_prev = m_sc[hh]
        m_cur = jnp.maximum(m_prev, jnp.max(mx, axis=-1, keepdims=True))
        alpha = jnp.exp2(m_prev - m_cur)
        p = jnp.concatenate([jnp.exp2((blk - m_cur).astype(BF16)) for blk in blocks], axis=1)
        acc_sc[hh] = jnp.concatenate([alpha, alpha], axis=1) * acc_sc[hh] + _dot(p, v)
        m_sc[hh] = m_cur

    def step(src_sc, dst_sc, tile):
        for hh in range(nh):
            dst_sc[hh] = scores(hh, tile + 1)
            consume(hh, src_sc[hh], tile, None)

    def diagonal(src_sc):
        qc = lax.broadcasted_iota(I32, (tq, TK), 0) // CHUNK
        kc = lax.broadcasted_iota(I32, (tq, TK), 1) // CHUNK
        for hh in range(nh):
            consume(hh, src_sc[hh], qi, kc <= qc)

    for hh in range(nh):
        s0_sc[hh] = scores(hh, 0)

    def body(j, carry):
        step(s0_sc, s1_sc, 2 * j)
        step(s1_sc, s0_sc, 2 * j + 1)
        return carry

    lax.fori_loop(0, qi // 2, body, 0)

    @pl.when(qi % 2 == 0)
    def _():
        diagonal(s0_sc)

    @pl.when(qi % 2 == 1)
    def _():
        step(s0_sc, s1_sc, qi - 1)
        diagonal(s1_sc)

    for hh in range(nh):
        acc = acc_sc[hh]
        out = acc[:, :V_HEAD] / acc[:, V_HEAD:V_HEAD + 1]
        o_ref[0, :, hh * V_HEAD:(hh + 1) * V_HEAD] = out.astype(o_ref.dtype)


def _attention(qa, ka, va):
    B, H, S, _ = qa.shape
    nh = ATTN_HEADS_PER_STEP
    return pl.pallas_call(
        _attn_kernel,
        grid=(B, H // nh, S // TQ),
        in_specs=[
            pl.BlockSpec((1, nh, TQ, A_QK_PAD), lambda b, h, i: (b, h, i, 0)),
            pl.BlockSpec((1, nh, S, A_QK_PAD), lambda b, h, i: (b, h, 0, 0),
                         pipeline_mode=pl.Buffered(1)),
            pl.BlockSpec((1, nh, S, V_HEAD), lambda b, h, i: (b, h, 0, 0),
                         pipeline_mode=pl.Buffered(1)),
        ],
        out_specs=pl.BlockSpec((1, TQ, nh * V_HEAD), lambda b, h, i: (b, i, h)),
        out_shape=jax.ShapeDtypeStruct((B, S, H * V_HEAD), BF16),
        scratch_shapes=[
            pltpu.VMEM((nh, TQ, LANES), F32), pltpu.VMEM((nh, TQ, V_HEAD + LANES), F32),
            pltpu.VMEM((nh, TQ, TK), F32), pltpu.VMEM((nh, TQ, TK), F32),
        ],
        compiler_params=pltpu.CompilerParams(
            dimension_semantics=("arbitrary", "arbitrary", "arbitrary"),
            vmem_limit_bytes=VMEM_LIMIT),
        name="mla_attention",
    )(qa, ka, va)


_ROUTE_LANE0 = N_GROUPS


def _merge_kernel(x_ref, ym_ref, o_ref, ya_ref, gt_ref, mod_ref, g2_ref, wm_ref, wa_ref, wo_ref,
                  wrh_ref, wrl_ref, br_ref,
                  xo_ref, h2_ref, ri_ref, cnt_ref, carry_sc):
    tm = x_ref.shape[1]
    D = x_ref.shape[2]

    @pl.when((pl.program_id(0) == 0) & (pl.program_id(1) == 0))
    def _():
        carry_sc[...] = jnp.zeros_like(carry_sc)

    ym = ym_ref[0].astype(F32) * _sigmoid(o_ref[0].astype(F32))
    bm = _dot(ym.astype(BF16), wm_ref[...])
    ba = _dot(ya_ref[0], wa_ref[...])
    gt = gt_ref[0].astype(F32)
    merged = _sigmoid(gt[:, :D]) * bm + _sigmoid(gt[:, D:]) * ba
    out = _dot(merged.astype(BF16), wo_ref[...])
    xn = x_ref[0] + mod_ref[0, 2:3, :] * out
    xo_ref[0] = xn

    h2 = _rms(xn, g2_ref[...]) * (1.0 + mod_ref[0, 4:5, :]) + mod_ref[0, 3:4, :]
    h2_ref[0, :, :D] = h2

    hh, hl = _split_bf16(h2)
    logits = _dot(hh, wrh_ref[...]) + _dot(hl, wrh_ref[...]) + _dot(hh, wrl_ref[...]) + br_ref[...]
    lane = lax.broadcasted_iota(I32, (tm, LANES), 1).astype(F32)
    big = float(LANES)
    is_grp = lane < N_GROUPS
    gl = jnp.where(is_grp, logits, -jnp.inf)
    gmax = jnp.max(gl, axis=-1, keepdims=True)
    grp = jnp.min(jnp.where(gl == gmax, lane, big), axis=-1, keepdims=True)
    p_grp = 1.0 / jnp.sum(jnp.where(is_grp, jnp.exp(logits - gmax), 0.0), axis=-1, keepdims=True)
    lo = _ROUTE_LANE0 + grp * EXPERTS_PER_GROUP
    el = jnp.where((lane >= lo) & (lane < lo + EXPERTS_PER_GROUP), logits, -jnp.inf)
    m1 = jnp.max(el, axis=-1, keepdims=True)
    i1 = jnp.min(jnp.where(el == m1, lane, big), axis=-1, keepdims=True)
    el2 = jnp.where(lane == i1, -jnp.inf, el)
    m2 = jnp.max(el2, axis=-1, keepdims=True)
    i2 = jnp.min(jnp.where(el2 == m2, lane, big), axis=-1, keepdims=True)
    r = jnp.exp(m2 - m1)
    w1 = p_grp / (1.0 + r)
    w2 = p_grp * r / (1.0 + r)

    j1 = i1 - lo
    j2 = i2 - lo
    ja = jnp.minimum(j1, j2)
    jb = jnp.maximum(j1, j2)
    wa = jnp.where(j1 < j2, w1, w2)
    wb = jnp.where(j1 < j2, w2, w1)
    pair = ja * (2 * EXPERTS_PER_GROUP - 1 - ja) * 0.5 + (jb - ja - 1.0)
    bucket = grp * PAIRS_PER_GROUP + pair
    hot = lane == bucket
    onehot = jnp.where(hot, 1.0, 0.0)
    rows = lax.broadcasted_iota(I32, (tm, tm), 0)
    cols = lax.broadcasted_iota(I32, (tm, tm), 1)
    strict = jnp.where(cols < rows, 1.0, 0.0).astype(BF16)
    before = _dot(strict, onehot.astype(BF16)) + carry_sc[...]
    rank = jnp.sum(jnp.where(hot, before, 0.0), axis=-1, keepdims=True)
    carry = carry_sc[...] + jnp.sum(onehot, axis=0, keepdims=True)
    carry_sc[...] = carry
    cnt_ref[...] = carry

    route = jnp.where(lane == 0, bucket, jnp.where(lane == 1, rank, 0.0))
    ri_ref[0] = route.T[:SUBLANES, :].astype(I32)
    h2_ref[0, :, D:] = jnp.where(lane == 0, wa, jnp.where(lane == 1, wb, 0.0))


def _merge(x, ym, o, ya, gt, mod, g2, wm, wa, wo, wrh, wrl, br):
    B, S, D = x.shape
    tm = TM_PROJ
    tok = lambda c: pl.BlockSpec((1, tm, c), lambda b, s: (b, s, 0))
    out_shape = (
        jax.ShapeDtypeStruct((B, S, D), F32),
        jax.ShapeDtypeStruct((B, S, D + LANES), F32),
        jax.ShapeDtypeStruct((B, SUBLANES, S), I32),
        jax.ShapeDtypeStruct((1, LANES), F32),
    )
    out_specs = (tok(D), tok(D + LANES),
                 pl.BlockSpec((1, SUBLANES, tm), lambda b, s: (b, 0, s)),
                 pl.BlockSpec((1, LANES), lambda b, s: (0, 0)))
    in_specs = [
        tok(D), tok(M_V_WIDTH), tok(M_V_WIDTH), tok(A_HEADS * V_HEAD), tok(2 * D),
        pl.BlockSpec((1, N_MOD, D), lambda b, s: (b, 0, 0)),
        _const_spec((1, D)),
        _const_spec(wm.shape), _const_spec(wa.shape), _const_spec(wo.shape),
        _const_spec(wrh.shape), _const_spec(wrl.shape), _const_spec(br.shape),
    ]
    return pl.pallas_call(
        _merge_kernel,
        grid=(B, S // tm),
        in_specs=in_specs,
        out_specs=out_specs,
        out_shape=out_shape,
        scratch_shapes=[pltpu.VMEM((1, LANES), F32)],
        compiler_params=pltpu.CompilerParams(
            dimension_semantics=("arbitrary", "arbitrary"), vmem_limit_bytes=VMEM_LIMIT),
        name="merge_router",
    )(x, ym, o, ya, gt, mod, g2, wm, wa, wo, wrh, wrl, br)


def _row_copy(src, si, dst, di, sem):
    return pltpu.make_async_copy(src.at[pl.ds(si, 1), :], dst.at[pl.ds(di, 1), :], sem)


def _tail_copy(zero_sc, xg_ref, row, tsem):
    return pltpu.make_async_copy(zero_sc, xg_ref.at[pl.ds(row, MOE_ROWS), :], tsem)


def _dispatch_kernel(bstart_ref, padlo_ref, bend_ref, bkt_ref, rank_ref, h_ref, xg_ref, dest_ref,
                     zero_sc, sem, zsem, tsem):
    n = bkt_ref.shape[0]
    step = pl.program_id(0)
    used = bend_ref[N_BUCKETS - 1]
    n_tail = (xg_ref.shape[0] - used) // MOE_ROWS

    @pl.when(step == 0)
    def _():
        zero_sc[...] = jnp.zeros_like(zero_sc)

        def per_bucket(b, c):
            def one(r, cc):
                _row_copy(zero_sc, 0, xg_ref, r, zsem).start()
                return cc
            lax.fori_loop(padlo_ref[b], bend_ref[b], one, 0)
            return c
        lax.fori_loop(0, N_BUCKETS, per_bucket, 0)

        def tail(j, c):
            _tail_copy(zero_sc, xg_ref, pl.multiple_of(used + j * MOE_ROWS, MOE_ROWS), tsem).start()
            return c
        lax.fori_loop(0, n_tail, tail, 0)

    def issue(a, c):
        dst = bstart_ref[bkt_ref[a]] + rank_ref[a]
        dest_ref[a] = dst
        _row_copy(h_ref, a, xg_ref, dst, sem).start()
        return c
    lax.fori_loop(0, n, issue, 0, unroll=4)

    def drain(a, c):
        _row_copy(h_ref, 0, xg_ref, 0, sem).wait()
        return c
    lax.fori_loop(0, n, drain, 0, unroll=8)

    @pl.when(step == pl.num_programs(0) - 1)
    def _():
        n_fill = used - pl.num_programs(0) * n

        def zdrain(a, c):
            _row_copy(zero_sc, 0, xg_ref, 0, zsem).wait()
            return c
        lax.fori_loop(0, n_fill, zdrain, 0)

        def tdrain(j, c):
            _tail_copy(zero_sc, xg_ref, 0, tsem).wait()
            return c
        lax.fori_loop(0, n_tail, tdrain, 0)


def _dispatch(bstart, padlo, bend, bkt, rank, h2e, cap):
    T, W = h2e.shape
    n = DISPATCH_ROWS
    smem_blk = pl.BlockSpec((n,), lambda i, *_: (i,), memory_space=pltpu.SMEM)
    grid_spec = pltpu.PrefetchScalarGridSpec(
        num_scalar_prefetch=3,
        grid=(T // n,),
        in_specs=[smem_blk, smem_blk, pl.BlockSpec((n, W), lambda i, *_: (i, 0))],
        out_specs=[pl.BlockSpec(memory_space=pl.ANY), smem_blk],
        scratch_shapes=[pltpu.VMEM((MOE_ROWS, W), F32), pltpu.SemaphoreType.DMA,
                        pltpu.SemaphoreType.DMA, pltpu.SemaphoreType.DMA],
    )
    return pl.pallas_call(
        _dispatch_kernel,
        grid_spec=grid_spec,
        out_shape=[jax.ShapeDtypeStruct((cap, W), F32),
                   jax.ShapeDtypeStruct((T,), I32)],
        compiler_params=pltpu.CompilerParams(
            dimension_semantics=("arbitrary",), vmem_limit_bytes=VMEM_LIMIT),
        name="moe_dispatch",
    )(bstart, padlo, bend, bkt, rank, h2e)


def _expert_kernel(blk_lo_ref, blk_hi_ref, nused_ref, x_ref, wgu_lo_ref, wd_lo_ref, wgu_hi_ref,
                   wd_hi_ref, o_ref):
    i = pl.program_id(0)
    D = o_ref.shape[1]

    @pl.when(i < nused_ref[0])
    def _():
        xb = x_ref[:, :D].astype(BF16)
        w = x_ref[:, D:]

        def ffn(wgu_ref, wd_ref):
            gu = _dot(xb, wgu_ref[0])
            a = _silu(gu[:, :D_EXPERT]) * gu[:, D_EXPERT:]
            return _dot(a.astype(BF16), wd_ref[0])

        o_ref[...] = w[:, 0:1] * ffn(wgu_lo_ref, wd_lo_ref) + w[:, 1:2] * ffn(wgu_hi_ref, wd_hi_ref)

    @pl.when(i >= nused_ref[0])
    def _():
        o_ref[...] = jnp.zeros_like(o_ref)


def _experts(blk_lo, blk_hi, nused, xg, wgu, wd):
    cap, W = xg.shape
    D = W - LANES
    R = MOE_ROWS
    grid_spec = pltpu.PrefetchScalarGridSpec(
        num_scalar_prefetch=3,
        grid=(cap // R,),
        in_specs=[
            pl.BlockSpec((R, W), lambda i, lo, hi, nu: (jnp.minimum(i, nu[0] - 1), 0)),
            pl.BlockSpec((1, D, 2 * D_EXPERT), lambda i, lo, hi, nu: (lo[i], 0, 0)),
            pl.BlockSpec((1, D_EXPERT, D), lambda i, lo, hi, nu: (lo[i], 0, 0)),
            pl.BlockSpec((1, D, 2 * D_EXPERT), lambda i, lo, hi, nu: (hi[i], 0, 0)),
            pl.BlockSpec((1, D_EXPERT, D), lambda i, lo, hi, nu: (hi[i], 0, 0)),
        ],
        out_specs=pl.BlockSpec((R, D), lambda i, lo, hi, nu: (i, 0)),
    )
    return pl.pallas_call(
        _expert_kernel,
        grid_spec=grid_spec,
        out_shape=jax.ShapeDtypeStruct((cap, D), F32),
        compiler_params=pltpu.CompilerParams(
            dimension_semantics=("arbitrary",), vmem_limit_bytes=VMEM_LIMIT),
        name="moe_experts",
    )(blk_lo, blk_hi, nused, xg, wgu, wd, wgu, wd)


def _combine_kernel(dest_ref, og_ref, x_ref, mod_ref, fg_ref, o_ref, buf_sc, sem, *, final_norm):
    tb = x_ref.shape[1]

    def issue(t, c):
        _row_copy(og_ref, dest_ref[t], buf_sc, t, sem).start()
        return c
    lax.fori_loop(0, tb, issue, 0, unroll=4)

    def drain(t, c):
        _row_copy(og_ref, 0, buf_sc, 0, sem).wait()
        return c
    lax.fori_loop(0, tb, drain, 0, unroll=8)

    xo = x_ref[0] + mod_ref[0, 5:6, :] * buf_sc[...]
    if final_norm:
        xo = _rms(xo, fg_ref[...])
    o_ref[0] = xo


def _combine(dest, outg, x, mod, fg, final_norm):
    B, S, D = x.shape
    tb = TB_COMBINE
    ns = S // tb
    return pl.pallas_call(
        functools.partial(_combine_kernel, final_norm=final_norm),
        grid=(B, ns),
        in_specs=[
            pl.BlockSpec((tb,), lambda b, s: (b * ns + s,), memory_space=pltpu.SMEM),
            pl.BlockSpec(memory_space=pl.ANY),
            pl.BlockSpec((1, tb, D), lambda b, s: (b, s, 0)),
            pl.BlockSpec((1, N_MOD, D), lambda b, s: (b, 0, 0)),
            pl.BlockSpec((1, D), lambda b, s: (0, 0)),
        ],
        out_specs=pl.BlockSpec((1, tb, D), lambda b, s: (b, s, 0)),
        out_shape=jax.ShapeDtypeStruct((B, S, D), F32),
        scratch_shapes=[pltpu.VMEM((tb, D), F32), pltpu.SemaphoreType.DMA],
        compiler_params=pltpu.CompilerParams(
            dimension_semantics=("arbitrary", "arbitrary"), vmem_limit_bytes=VMEM_LIMIT),
        name="moe_combine",
    )(dest, outg, x, mod, fg)


def _rotate_half_cols(w):
    half = w.shape[-1] // 2
    return jnp.concatenate([-w[..., half:], w[..., :half]], axis=-1)


def _pack_layer_weights(l, w_in, w_uq, w_ukv, w_group, w_router, b_group, b_router,
                        w_expert_gate, w_expert_up, igate_b, fgate_b):
    D = w_in.shape[1]
    sizes = (M_QK_WIDTH, M_QK_WIDTH, M_V_WIDTH, M_V_WIDTH, M_HEADS, M_HEADS, Q_LORA, KV_LORA,
             QK_ROPE, 2 * D)
    pts = [sum(sizes[:i + 1]) for i in range(len(sizes) - 1)]
    wq, wk, wv, wo, wi, wf, wcq, wckv, wkr, wg = jnp.split(w_in[l], pts, axis=1)
    wmain = jnp.concatenate([wq, wk, wv, wo, wkr, _rotate_half_cols(wkr), wcq, wckv, wg],
                            axis=1).astype(BF16)
    wif = jnp.concatenate([wi, wf], axis=1)
    wifc = jnp.pad(wif, ((0, 0), (0, LANES - 2 * M_HEADS))).astype(BF16)
    wift = wif.T.astype(BF16)

    uq = w_uq[l].reshape(Q_LORA, A_HEADS, QK_NOPE + QK_ROPE)
    uq_rope = uq[:, :, QK_NOPE:]
    wuq = jnp.concatenate([uq[:, :, :QK_NOPE], uq_rope, _rotate_half_cols(uq_rope)], axis=-1)
    wuq = wuq.reshape(Q_LORA, A_HEADS * A_QK_PAD).astype(BF16)
    ukv = w_ukv[l].reshape(KV_LORA, A_HEADS, QK_NOPE + V_HEAD)
    wkn = ukv[:, :, :QK_NOPE].reshape(KV_LORA, A_HEADS * QK_NOPE).astype(BF16)
    wvv = ukv[:, :, QK_NOPE:].reshape(KV_LORA, A_HEADS * V_HEAD).astype(BF16)

    wr = jnp.concatenate([w_group[l], w_router[l]], axis=1)
    wr = jnp.pad(wr, ((0, 0), (0, LANES - wr.shape[1])))
    wrh = wr.astype(BF16)
    wrl = (wr - wrh.astype(F32)).astype(BF16)
    br = jnp.concatenate([b_group[l], b_router[l]])
    br = jnp.pad(br, (0, LANES - br.shape[0])).reshape(1, LANES)

    wgu = jnp.concatenate([w_expert_gate[l], w_expert_up[l]], axis=-1).astype(BF16)

    gb = jnp.concatenate([igate_b[l], fgate_b[l]])
    bias_c = jnp.pad(gb, (0, LANES - gb.shape[0])).reshape(1, LANES)
    bias_r = jnp.broadcast_to(gb[:, None], (2 * M_HEADS, LANES))
    return wmain, wifc, wift, wuq, wkn, wvv, wrh, wrl, br, wgu, bias_c, bias_r


def kernel(x, c, mod_w, mod_b, norm1_g, w_in, conv_w, conv_b, igate_b, fgate_b, q_norm_g,
           kv_norm_g, w_uq, w_ukv, w_branch_m, w_branch_a, w_out, norm2_g, w_group, b_group,
           w_router, b_router, w_expert_gate, w_expert_up, w_expert_down, final_norm_g):
    B, S, D = x.shape
    L = mod_w.shape[0]
    T = B * S
    assert S % TQ == 0 and TM_PROJ % MLSTM_CHUNK == 0 and S % TM_PROJ == 0 and S % TB_COMBINE == 0
    assert TQ == TK and TQ % CHUNK == 0 and T % DISPATCH_ROWS == 0
    assert A_HEADS % ATTN_HEADS_PER_STEP == 0 and V_HEAD == LANES
    assert N_BUCKETS <= LANES and TOP_K == 2

    pos = jnp.arange(S, dtype=F32)
    inv_freq = 1.0 / (ROPE_THETA ** (jnp.arange(0, QK_ROPE, 2, dtype=F32) / QK_ROPE))
    ang = pos[:, None] * inv_freq[None, :]
    cos = jnp.cos(ang)
    sin = jnp.sin(ang)
    z = jnp.zeros((S, QK_ROPE), F32)
    cs = jnp.concatenate([cos, cos, z, sin, sin, z], axis=1)

    mod_all = _modulation(c, mod_w, mod_b).reshape(L, B, N_MOD, D)

    cap = T + N_BUCKETS * MOE_ROWS
    n_blocks = cap // MOE_ROWS
    pairs = [(a, b) for a in range(EXPERTS_PER_GROUP) for b in range(a + 1, EXPERTS_PER_GROUP)]
    bucket_lo = jnp.array([g * EXPERTS_PER_GROUP + a for g in range(N_GROUPS) for a, _ in pairs], I32)
    bucket_hi = jnp.array([g * EXPERTS_PER_GROUP + b for g in range(N_GROUPS) for _, b in pairs], I32)

    for l in range(L):
        (wmain, wifc, wift, wuq, wkn, wvv, wrh, wrl, br, wgu, bias_c, bias_r) = _pack_layer_weights(
            l, w_in, w_uq, w_ukv, w_group, w_router, b_group, b_router, w_expert_gate,
            w_expert_up, igate_b, fgate_b)
        mod = mod_all[l]

        ym, o, gt, qa, ka, va = _in_proj(
            x, mod, norm1_g[l].reshape(1, D), wmain, wifc, wift, conv_w[l],
            conv_b[l].reshape(1, -1), q_norm_g[l].reshape(1, -1), kv_norm_g[l].reshape(1, -1),
            wuq, wkn, wvv, cs, bias_c, bias_r)
        ya = _attention(qa, ka, va)
        xn, h2e, ri, cnt = _merge(
            x, ym, o, ya, gt, mod, norm2_g[l].reshape(1, D), w_branch_m[l].astype(BF16),
            w_branch_a[l].astype(BF16), w_out[l].astype(BF16), wrh, wrl, br)

        counts = cnt[0, :N_BUCKETS].astype(I32)
        padded = (counts + MOE_ROWS - 1) // MOE_ROWS * MOE_ROWS
        bend = jnp.cumsum(padded).astype(I32)
        bstart = bend - padded
        padlo = bstart + counts
        blk_start = jnp.arange(n_blocks, dtype=I32) * MOE_ROWS
        blk_b = jnp.minimum(jnp.sum(bend[None, :] <= blk_start[:, None], axis=1), N_BUCKETS - 1)
        nused = (bend[N_BUCKETS - 1:] // MOE_ROWS).astype(I32)
        bkt = ri[:, 0, :].reshape(T)
        rank = ri[:, 1, :].reshape(T)

        xg, dest = _dispatch(bstart, padlo, bend, bkt, rank, h2e.reshape(T, D + LANES), cap)
        outg = _experts(bucket_lo[blk_b], bucket_hi[blk_b], nused, xg, wgu,
                        w_expert_down[l].astype(BF16))
        x = _combine(dest, outg, xn, mod, final_norm_g.reshape(1, D), final_norm=(l == L - 1))
    return x
```

```python
import functools
import math

import jax
import jax.numpy as jnp
from jax import lax
from jax.experimental import pallas as pl
from jax.experimental.pallas import tpu as pltpu

F32 = jnp.float32
BF16 = jnp.bfloat16
I32 = jnp.int32

EPS = 1e-6
CHUNK = 64
N_MOD = 6

M_HEADS = 4
M_QK_DIM = 128
M_V_DIM = 256
M_QK_WIDTH = M_HEADS * M_QK_DIM
M_V_WIDTH = M_HEADS * M_V_DIM
CONV_WIDTH = 4

A_HEADS = 8
Q_LORA = 384
KV_LORA = 256
QK_NOPE = 128
QK_ROPE = 64
V_HEAD = 128
A_QK_PAD = 256
ROPE_THETA = 10000.0

N_GROUPS = 4
EXPERTS_PER_GROUP = 8
N_EXPERTS = N_GROUPS * EXPERTS_PER_GROUP
PAIRS_PER_GROUP = EXPERTS_PER_GROUP * (EXPERTS_PER_GROUP - 1) // 2
N_BUCKETS = N_GROUPS * PAIRS_PER_GROUP
TOP_K = 2
D_EXPERT = 256

LANES = 128
SUBLANES = 8

TM_PROJ = 512
MLSTM_CHUNK = 256
TQ = 512
TK = 512
ATTN_HEADS_PER_STEP = 4
MOE_ROWS = 256
DISPATCH_ROWS = 2048
TB_COMBINE = 512

VMEM_LIMIT = 56 * 1024 * 1024

_OFF_QK = 0
_OFF_V = _OFF_QK + 2 * M_QK_WIDTH
_OFF_O = _OFF_V + M_V_WIDTH
_OFF_KR = _OFF_O + M_V_WIDTH
_OFF_CQ = _OFF_KR + 2 * QK_ROPE
_OFF_CKV = _OFF_CQ + Q_LORA
_OFF_G = _OFF_CKV + KV_LORA
_N_MAIN = _OFF_G + 2 * 1024


def _dot(a, b):
    return jnp.dot(a, b, preferred_element_type=F32)


def _dot_nt(a, b):
    return lax.dot_general(a, b, (((1,), (1,)), ((), ())), preferred_element_type=F32)


def _dot_tn(a, b):
    return lax.dot_general(a, b, (((0,), (0,)), ((), ())), preferred_element_type=F32)


def _split_bf16(x):
    hi = x.astype(BF16)
    lo = (x - hi.astype(F32)).astype(BF16)
    return hi, lo


def _sigmoid(x):
    return 1.0 / (1.0 + jnp.exp(-x))


def _silu(x):
    return x * _sigmoid(x)


def _rms(x, g):
    return x * lax.rsqrt(jnp.mean(x * x, axis=-1, keepdims=True) + EPS) * g


def _const_spec(shape):
    nd = len(shape)
    return pl.BlockSpec(shape, lambda *_: (0,) * nd, pipeline_mode=pl.Buffered(1))


def _mod_kernel(c_ref, w_ref, b_ref, o_ref):
    c = c_ref[...]
    cond = _silu(c)
    ch, cl = _split_bf16(cond)
    wh, wl = _split_bf16(w_ref[0])
    o_ref[0] = _dot(ch, wh) + _dot(cl, wh) + _dot(ch, wl) + b_ref[0]


def _modulation(c, mod_w, mod_b):
    L, D, N = mod_w.shape
    B = c.shape[0]
    tn = 1536
    return pl.pallas_call(
        _mod_kernel,
        grid=(L, N // tn),
        in_specs=[
            pl.BlockSpec((B, D), lambda l, j: (0, 0)),
            pl.BlockSpec((1, D, tn), lambda l, j: (l, 0, j)),
            pl.BlockSpec((1, 1, tn), lambda l, j: (l, 0, j)),
        ],
        out_specs=pl.BlockSpec((1, B, tn), lambda l, j: (l, 0, j)),
        out_shape=jax.ShapeDtypeStruct((L, B, N), F32),
        compiler_params=pltpu.CompilerParams(
            dimension_semantics=("arbitrary", "arbitrary"), vmem_limit_bytes=VMEM_LIMIT),
        name="adaln_mod",
    )(c, mod_w, mod_b.reshape(L, 1, N))


def _rope128(blk, cs):
    return blk * cs[:, :LANES] + pltpu.roll(blk, QK_ROPE, 1) * cs[:, LANES:]


def _in_proj_kernel(x_ref, mod_ref, g_ref, w_ref, wifc_ref, wift_ref, cw_ref, cb_ref, qng_ref,
                    kvng_ref, wuq_ref, wkn_ref, wv_ref, cs_ref, bc_ref, br_ref,
                    ym_ref, o_ref, gt_ref, qa_ref, ka_ref, va_ref,
                    tail_sc, c_sc, n_sc, m_sc):
    tm = x_ref.shape[1]

    @pl.when(pl.program_id(1) == 0)
    def _():
        tail_sc[...] = jnp.zeros_like(tail_sc)
        c_sc[...] = jnp.zeros_like(c_sc)
        n_sc[...] = jnp.zeros_like(n_sc)
        m_sc[...] = jnp.zeros_like(m_sc)

    x = x_ref[0]
    shift = mod_ref[0, 0:1, :]
    scale = mod_ref[0, 1:2, :]
    h = _rms(x, g_ref[...]) * (1.0 + scale) + shift
    hb = h.astype(BF16)

    raw = _dot(hb, w_ref[:, _OFF_QK:_OFF_V])
    tail = tail_sc[...]
    row8 = lax.broadcasted_iota(I32, (SUBLANES, 1), 0)
    acc = cb_ref[...] + cw_ref[CONV_WIDTH - 1:CONV_WIDTH, :] * raw
    for j in range(1, CONV_WIDTH):
        sh = pltpu.roll(raw, j, 0)
        fix = pltpu.roll(tail, j, 0)
        first = jnp.where(row8 < j, fix, sh[:SUBLANES])
        sh = jnp.concatenate([first, sh[SUBLANES:]], axis=0)
        acc = acc + cw_ref[CONV_WIDTH - 1 - j:CONV_WIDTH - j, :] * sh
    tail_sc[...] = raw[tm - SUBLANES:, :]
    qk = _silu(acc)
    q_m = (qk[:, :M_QK_WIDTH] * (M_QK_DIM ** -0.5)).astype(BF16)
    k_m = qk[:, M_QK_WIDTH:].astype(BF16)
    v_m = _dot(hb, w_ref[:, _OFF_V:_OFF_O]).astype(BF16)
    gcol = _dot(hb, wifc_ref[...]) + bc_ref[...]
    grow = _dot_nt(wift_ref[...], hb) + br_ref[:, :1]
    for c in range(tm // MLSTM_CHUNK):
        r0, r1 = c * MLSTM_CHUNK, (c + 1) * MLSTM_CHUNK
        _mlstm_chunk(q_m[r0:r1], k_m[r0:r1], v_m[r0:r1], gcol[r0:r1], grow[:, r0:r1],
                     ym_ref, r0, c_sc, n_sc, m_sc)

    o_ref[0] = _dot(hb, w_ref[:, _OFF_O:_OFF_KR]).astype(BF16)
    gt_ref[0] = _dot(hb, w_ref[:, _OFF_G:_N_MAIN]).astype(BF16)

    cs = cs_ref[...]
    kr = _rope128(_dot(hb, w_ref[:, _OFF_KR:_OFF_CQ]), cs).astype(BF16)

    cq = _dot(hb, w_ref[:, _OFF_CQ:_OFF_CKV])
    cqn = _rms(cq, qng_ref[...]).astype(BF16)
    qa = _dot(cqn, wuq_ref[...])
    att_scale = (QK_NOPE + QK_ROPE) ** -0.5 * math.log2(math.e)
    for hh in range(A_HEADS):
        base = hh * A_QK_PAD
        qa_ref[0, hh, :, :QK_NOPE] = (qa[:, base:base + QK_NOPE] * att_scale).astype(BF16)
        roped = _rope128(qa[:, base + QK_NOPE:base + A_QK_PAD], cs)
        qa_ref[0, hh, :, QK_NOPE:] = (roped * att_scale).astype(BF16)

    ckv = _dot(hb, w_ref[:, _OFF_CKV:_OFF_G])
    ckvn = _rms(ckv, kvng_ref[...]).astype(BF16)
    kn = _dot(ckvn, wkn_ref[...]).astype(BF16)
    vv = _dot(ckvn, wv_ref[...]).astype(BF16)
    for hh in range(A_HEADS):
        ka_ref[0, hh, :, :QK_NOPE] = kn[:, hh * QK_NOPE:(hh + 1) * QK_NOPE]
        ka_ref[0, hh, :, QK_NOPE:] = kr
        va_ref[0, hh] = vv[:, hh * V_HEAD:(hh + 1) * V_HEAD]


def _in_proj(x, mod, g, wmain, wifc, wift, conv_w, conv_b, qng, kvng, wuq, wkn, wv, cs, bias_c,
             bias_r):
    B, S, D = x.shape
    tm = TM_PROJ
    ns = S // tm
    tok = lambda c: pl.BlockSpec((1, tm, c), lambda b, s: (b, s, 0))
    head = lambda c: pl.BlockSpec((1, A_HEADS, tm, c), lambda b, s: (b, 0, s, 0))
    out_shape = (
        jax.ShapeDtypeStruct((B, S, M_V_WIDTH), BF16),
        jax.ShapeDtypeStruct((B, S, M_V_WIDTH), BF16),
        jax.ShapeDtypeStruct((B, S, 2 * D), BF16),
        jax.ShapeDtypeStruct((B, A_HEADS, S, A_QK_PAD), BF16),
        jax.ShapeDtypeStruct((B, A_HEADS, S, A_QK_PAD), BF16),
        jax.ShapeDtypeStruct((B, A_HEADS, S, V_HEAD), BF16),
    )
    out_specs = (
        tok(M_V_WIDTH), tok(M_V_WIDTH), tok(2 * D), head(A_QK_PAD), head(A_QK_PAD), head(V_HEAD),
    )
    in_specs = [
        tok(D),
        pl.BlockSpec((1, N_MOD, D), lambda b, s: (b, 0, 0)),
        _const_spec((1, D)),
        _const_spec(wmain.shape), _const_spec(wifc.shape), _const_spec(wift.shape),
        _const_spec(conv_w.shape), _const_spec(conv_b.shape),
        _const_spec(qng.shape), _const_spec(kvng.shape),
        _const_spec(wuq.shape), _const_spec(wkn.shape), _const_spec(wv.shape),
        pl.BlockSpec((tm, 2 * LANES), lambda b, s: (s, 0)),
        _const_spec(bias_c.shape), _const_spec(bias_r.shape),
    ]
    return pl.pallas_call(
        _in_proj_kernel,
        grid=(B, ns),
        in_specs=in_specs,
        out_specs=out_specs,
        out_shape=out_shape,
        scratch_shapes=[
            pltpu.VMEM((SUBLANES, 2 * M_QK_WIDTH), F32),
            pltpu.VMEM((M_HEADS, M_QK_DIM, M_V_DIM), F32),
            pltpu.VMEM((M_HEADS, 1, M_QK_DIM), F32),
            pltpu.VMEM((M_HEADS, 1, LANES), F32),
        ],
        compiler_params=pltpu.CompilerParams(
            dimension_semantics=("arbitrary", "arbitrary"), vmem_limit_bytes=VMEM_LIMIT),
        name="in_proj",
    )(x, mod, g, wmain, wifc, wift, conv_w, conv_b, qng, kvng, wuq, wkn, wv, cs, bias_c, bias_r)


def _log_sigmoid(x):
    return jnp.minimum(x, 0.0) - jnp.log(1.0 + jnp.exp(-jnp.abs(x)))


def _mlstm_chunk(q_all, k_all, v_all, gcol, grow, y_ref, row0, c_sc, n_sc, m_sc):
    L = q_all.shape[0]
    rows = lax.broadcasted_iota(I32, (L, L), 0)
    cols = lax.broadcasted_iota(I32, (L, L), 1)
    causal = cols <= rows
    tri = jnp.where(causal, 1.0, 0.0).astype(BF16)
    tri_t = jnp.where(rows <= cols, 1.0, 0.0).astype(BF16)

    ch, cl = _split_bf16(_log_sigmoid(gcol))
    b_col = _dot(tri, ch) + _dot(tri, cl)
    rh, rl = _split_bf16(_log_sigmoid(grow))
    b_row = _dot(rh, tri_t) + _dot(rl, tri_t)

    for hh in range(M_HEADS):
        q = q_all[:, hh * M_QK_DIM:(hh + 1) * M_QK_DIM]
        k = k_all[:, hh * M_QK_DIM:(hh + 1) * M_QK_DIM]
        v = v_all[:, hh * M_V_DIM:(hh + 1) * M_V_DIM]
        li_c = gcol[:, hh:hh + 1]
        b_c = b_col[:, M_HEADS + hh:M_HEADS + hh + 1]
        li_r = grow[hh:hh + 1, :]
        b_r = b_row[M_HEADS + hh:M_HEADS + hh + 1, :]
        m_prev = m_sc[hh][:, :1]
        c_prev = c_sc[hh]
        n_prev = n_sc[hh]

        dmat = jnp.where(causal, b_c - b_r + li_r, -jnp.inf)
        m_inter = b_c + m_prev
        m_t = jnp.maximum(jnp.max(dmat, axis=-1, keepdims=True), m_inter)
        s = _dot_nt(q, k) * jnp.exp(dmat - m_t)
        w_inter = jnp.exp(m_inter - m_t)
        num = _dot(s.astype(BF16), v) + w_inter * _dot(q, c_prev.astype(BF16))
        qn = jnp.sum(q.astype(F32) * n_prev, axis=-1, keepdims=True)
        den = jnp.sum(s, axis=-1, keepdims=True) + w_inter * qn
        hout = num / jnp.maximum(jnp.abs(den), jnp.exp(-m_t))
        y_ref[0, row0:row0 + L, hh * M_V_DIM:(hh + 1) * M_V_DIM] = hout.astype(y_ref.dtype)

        b_last = b_c[L - 1:L, :]
        a_c = b_last - b_c + li_c
        m_new = jnp.maximum(b_last + m_prev, jnp.max(a_c, axis=0, keepdims=True))
        decay = jnp.exp(b_last + m_prev - m_new)
        wk = jnp.exp(a_c - m_new)
        wv = (wk * v.astype(F32)).astype(BF16)
        c_sc[hh] = decay * c_prev + _dot_tn(k, wv)
        n_sc[hh] = decay * n_prev + jnp.sum(wk * k.astype(F32), axis=0, keepdims=True)
        m_sc[hh] = jnp.broadcast_to(m_new, (1, LANES))


def _attn_kernel(q_ref, k_ref, v_ref, o_ref, m_sc, acc_sc, s0_sc, s1_sc):
    nh = q_ref.shape[1]
    tq = q_ref.shape[2]
    qi = pl.program_id(2)
    nblk = TK // LANES

    m_sc[...] = jnp.full_like(m_sc, -jnp.inf)
    acc_sc[...] = jnp.zeros_like(acc_sc)
    ones_col = jnp.where(lax.broadcasted_iota(I32, (TK, LANES), 1) == 0, 1.0, 0.0).astype(BF16)

    def scores(hh, tile):
        k = k_ref[0, hh, pl.ds(pl.multiple_of(tile * TK, TK), TK), :]
        return _dot_nt(q_ref[0, hh], k)

    def consume(hh, s, tile, mask):
        v = v_ref[0, hh, pl.ds(pl.multiple_of(tile * TK, TK), TK), :]
        v = jnp.concatenate([v, ones_col], axis=1)
        if mask is not None:
            s = jnp.where(mask, s, -jnp.inf)
        blocks = [s[:, c * LANES:(c + 1) * LANES] for c in range(nblk)]
        mx = blocks[0]
        for blk in blocks[1:]:
            mx = jnp.maximum(mx, blk)
        m_prev = m_sc[hh]
        m_cur = jnp.maximum(m_prev, jnp.max(mx, axis=-1, keepdims=True))
        alpha = jnp.exp2(m_prev - m_cur)
        p = jnp.concatenate([jnp.exp2((blk - m_cur).astype(BF16)) for blk in blocks], axis=1)
        acc_sc[hh] = jnp.concatenate([alpha, alpha], axis=1) * acc_sc[hh] + _dot(p, v)
        m_sc[hh] = m_cur

    def step(src_sc, dst_sc, tile):
        for hh in range(nh):
            dst_sc[hh] = scores(hh, tile + 1)
            consume(hh, src_sc[hh], tile, None)

    def diagonal(src_sc):
        qc = lax.broadcasted_iota(I32, (tq, TK), 0) // CHUNK
        kc = lax.broadcasted_iota(I32, (tq, TK), 1) // CHUNK
        for hh in range(nh):
            consume(hh, src_sc[hh], qi, kc <= qc)

    for hh in range(nh):
        s0_sc[hh] = scores(hh, 0)

    def body(j, carry):
        step(s0_sc, s1_sc, 2 * j)
        step(s1_sc, s0_sc, 2 * j + 1)
        return carry

    lax.fori_loop(0, qi // 2, body, 0)

    @pl.when(qi % 2 == 0)
    def _():
        diagonal(s0_sc)

    @pl.when(qi % 2 == 1)
    def _():
        step(s0_sc, s1_sc, qi - 1)
        diagonal(s1_sc)

    for hh in range(nh):
        acc = acc_sc[hh]
        out = acc[:, :V_HEAD] / acc[:, V_HEAD:V_HEAD + 1]
        o_ref[0, :, hh * V_HEAD:(hh + 1) * V_HEAD] = out.astype(o_ref.dtype)


def _attention(qa, ka, va):
    B, H, S, _ = qa.shape
    nh = ATTN_HEADS_PER_STEP
    return pl.pallas_call(
        _attn_kernel,
        grid=(B, H // nh, S // TQ),
        in_specs=[
            pl.BlockSpec((1, nh, TQ, A_QK_PAD), lambda b, h, i: (b, h, i, 0)),
            pl.BlockSpec((1, nh, S, A_QK_PAD), lambda b, h, i: (b, h, 0, 0),
                         pipeline_mode=pl.Buffered(1)),
            pl.BlockSpec((1, nh, S, V_HEAD), lambda b, h, i: (b, h, 0, 0),
                         pipeline_mode=pl.Buffered(1)),
        ],
        out_specs=pl.BlockSpec((1, TQ, nh * V_HEAD), lambda b, h, i: (b, i, h)),
        out_shape=jax.ShapeDtypeStruct((B, S, H * V_HEAD), BF16),
        scratch_shapes=[
            pltpu.VMEM((nh, TQ, LANES), F32), pltpu.VMEM((nh, TQ, V_HEAD + LANES), F32),
            pltpu.VMEM((nh, TQ, TK), F32), pltpu.VMEM((nh, TQ, TK), F32),
        ],
        compiler_params=pltpu.CompilerParams(
            dimension_semantics=("arbitrary", "arbitrary", "arbitrary"),
            vmem_limit_bytes=VMEM_LIMIT),
        name="mla_attention",
    )(qa, ka, va)


_ROUTE_LANE0 = N_GROUPS


def _merge_kernel(x_ref, ym_ref, o_ref, ya_ref, gt_ref, mod_ref, g2_ref, wm_ref, wa_ref, wo_ref,
                  wrh_ref, wrl_ref, br_ref,
                  xo_ref, h2_ref, ri_ref, cnt_ref, carry_sc):
    tm = x_ref.shape[1]
    D = x_ref.shape[2]

    @pl.when((pl.program_id(0) == 0) & (pl.program_id(1) == 0))
    def _():
        carry_sc[...] = jnp.zeros_like(carry_sc)

    ym = ym_ref[0].astype(F32) * _sigmoid(o_ref[0].astype(F32))
    bm = _dot(ym.astype(BF16), wm_ref[...])
    ba = _dot(ya_ref[0], wa_ref[...])
    gt = gt_ref[0].astype(F32)
    merged = _sigmoid(gt[:, :D]) * bm + _sigmoid(gt[:, D:]) * ba
    out = _dot(merged.astype(BF16), wo_ref[...])
    xn = x_ref[0] + mod_ref[0, 2:3, :] * out
    xo_ref[0] = xn

    h2 = _rms(xn, g2_ref[...]) * (1.0 + mod_ref[0, 4:5, :]) + mod_ref[0, 3:4, :]
    h2_ref[0, :, :D] = h2

    hh, hl = _split_bf16(h2)
    logits = _dot(hh, wrh_ref[...]) + _dot(hl, wrh_ref[...]) + _dot(hh, wrl_ref[...]) + br_ref[...]
    lane = lax.broadcasted_iota(I32, (tm, LANES), 1).astype(F32)
    big = float(LANES)
    is_grp = lane < N_GROUPS
    gl = jnp.where(is_grp, logits, -jnp.inf)
    gmax = jnp.max(gl, axis=-1, keepdims=True)
    grp = jnp.min(jnp.where(gl == gmax, lane, big), axis=-1, keepdims=True)
    p_grp = 1.0 / jnp.sum(jnp.where(is_grp, jnp.exp(logits - gmax), 0.0), axis=-1, keepdims=True)
    lo = _ROUTE_LANE0 + grp * EXPERTS_PER_GROUP
    el = jnp.where((lane >= lo) & (lane < lo + EXPERTS_PER_GROUP), logits, -jnp.inf)
    m1 = jnp.max(el, axis=-1, keepdims=True)
    i1 = jnp.min(jnp.where(el == m1, lane, big), axis=-1, keepdims=True)
    el2 = jnp.where(lane == i1, -jnp.inf, el)
    m2 = jnp.max(el2, axis=-1, keepdims=True)
    i2 = jnp.min(jnp.where(el2 == m2, lane, big), axis=-1, keepdims=True)
    r = jnp.exp(m2 - m1)
    w1 = p_grp / (1.0 + r)
    w2 = p_grp * r / (1.0 + r)

    j1 = i1 - lo
    j2 = i2 - lo
    ja = jnp.minimum(j1, j2)
    jb = jnp.maximum(j1, j2)
    wa = jnp.where(j1 < j2, w1, w2)
    wb = jnp.where(j1 < j2, w2, w1)
    pair = ja * (2 * EXPERTS_PER_GROUP - 1 - ja) * 0.5 + (jb - ja - 1.0)
    bucket = grp * PAIRS_PER_GROUP + pair
    hot = lane == bucket
    onehot = jnp.where(hot, 1.0, 0.0)
    rows = lax.broadcasted_iota(I32, (tm, tm), 0)
    cols = lax.broadcasted_iota(I32, (tm, tm), 1)
    strict = jnp.where(cols < rows, 1.0, 0.0).astype(BF16)
    before = _dot(strict, onehot.astype(BF16)) + carry_sc[...]
    rank = jnp.sum(jnp.where(hot, before, 0.0), axis=-1, keepdims=True)
    carry = carry_sc[...] + jnp.sum(onehot, axis=0, keepdims=True)
    carry_sc[...] = carry
    cnt_ref[...] = carry

    route = jnp.where(lane == 0, bucket, jnp.where(lane == 1, rank, 0.0))
    ri_ref[0] = route.T[:SUBLANES, :].astype(I32)
    h2_ref[0, :, D:] = jnp.where(lane == 0, wa, jnp.where(lane == 1, wb, 0.0))


def _merge(x, ym, o, ya, gt, mod, g2, wm, wa, wo, wrh, wrl, br):
    B, S, D = x.shape
    tm = TM_PROJ
    tok = lambda c: pl.BlockSpec((1, tm, c), lambda b, s: (b, s, 0))
    out_shape = (
        jax.ShapeDtypeStruct((B, S, D), F32),
        jax.ShapeDtypeStruct((B, S, D + LANES), F32),
        jax.ShapeDtypeStruct((B, SUBLANES, S), I32),
        jax.ShapeDtypeStruct((1, LANES), F32),
    )
    out_specs = (tok(D), tok(D + LANES),
                 pl.BlockSpec((1, SUBLANES, tm), lambda b, s: (b, 0, s)),
                 pl.BlockSpec((1, LANES), lambda b, s: (0, 0)))
    in_specs = [
        tok(D), tok(M_V_WIDTH), tok(M_V_WIDTH), tok(A_HEADS * V_HEAD), tok(2 * D),
        pl.BlockSpec((1, N_MOD, D), lambda b, s: (b, 0, 0)),
        _const_spec((1, D)),
        _const_spec(wm.shape), _const_spec(wa.shape), _const_spec(wo.shape),
        _const_spec(wrh.shape), _const_spec(wrl.shape), _const_spec(br.shape),
    ]
    return pl.pallas_call(
        _merge_kernel,
        grid=(B, S // tm),
        in_specs=in_specs,
        out_specs=out_specs,
        out_shape=out_shape,
        scratch_shapes=[pltpu.VMEM((1, LANES), F32)],
        compiler_params=pltpu.CompilerParams(
            dimension_semantics=("arbitrary", "arbitrary"), vmem_limit_bytes=VMEM_LIMIT),
        name="merge_router",
    )(x, ym, o, ya, gt, mod, g2, wm, wa, wo, wrh, wrl, br)


def _row_copy(src, si, dst, di, sem):
    return pltpu.make_async_copy(src.at[pl.ds(si, 1), :], dst.at[pl.ds(di, 1), :], sem)


def _zero_block_copy(zero_sc, xg_ref, row, zsem):
    return pltpu.make_async_copy(zero_sc, xg_ref.at[pl.ds(row, MOE_ROWS), :], zsem)


def _dispatch_kernel(bstart_ref, bend_ref, bkt_ref, rank_ref, h_ref, xg_ref, dest_ref,
                     zero_sc, sem, zsem):
    n = bkt_ref.shape[0]
    step = pl.program_id(0)

    @pl.when(step == 0)
    def _():
        zero_sc[...] = jnp.zeros_like(zero_sc)
        used = bend_ref[N_BUCKETS - 1]
        n_tail = (xg_ref.shape[0] - used) // MOE_ROWS

        def per_bucket(b, cnt):
            nonempty = bend_ref[b] > bstart_ref[b]

            @pl.when(nonempty)
            def _():
                row = pl.multiple_of(bend_ref[b] - MOE_ROWS, MOE_ROWS)
                _zero_block_copy(zero_sc, xg_ref, row, zsem).start()
            return cnt + jnp.where(nonempty, 1, 0)
        n_last = lax.fori_loop(0, N_BUCKETS, per_bucket, jnp.int32(0))

        def tail(j, c):
            row = pl.multiple_of(used + j * MOE_ROWS, MOE_ROWS)
            _zero_block_copy(zero_sc, xg_ref, row, zsem).start()
            return c
        lax.fori_loop(0, n_tail, tail, 0)

        def zdrain(j, c):
            _zero_block_copy(zero_sc, xg_ref, 0, zsem).wait()
            return c
        lax.fori_loop(0, n_last + n_tail, zdrain, 0)

    def issue(a, c):
        dst = bstart_ref[bkt_ref[a]] + rank_ref[a]
        dest_ref[a] = dst
        _row_copy(h_ref, a, xg_ref, dst, sem).start()
        return c
    lax.fori_loop(0, n, issue, 0, unroll=8)

    def drain(a, c):
        _row_copy(h_ref, 0, xg_ref, 0, sem).wait()
        return c
    lax.fori_loop(0, n, drain, 0, unroll=8)


def _dispatch(bstart, bend, bkt, rank, h2e, cap):
    T, W = h2e.shape
    n = DISPATCH_ROWS
    smem_blk = pl.BlockSpec((n,), lambda i, *_: (i,), memory_space=pltpu.SMEM)
    grid_spec = pltpu.PrefetchScalarGridSpec(
        num_scalar_prefetch=2,
        grid=(T // n,),
        in_specs=[smem_blk, smem_blk, pl.BlockSpec((n, W), lambda i, *_: (i, 0))],
        out_specs=[pl.BlockSpec(memory_space=pl.ANY), smem_blk],
        scratch_shapes=[pltpu.VMEM((MOE_ROWS, W), F32), pltpu.SemaphoreType.DMA,
                        pltpu.SemaphoreType.DMA],
    )
    return pl.pallas_call(
        _dispatch_kernel,
        grid_spec=grid_spec,
        out_shape=[jax.ShapeDtypeStruct((cap, W), F32),
                   jax.ShapeDtypeStruct((T,), I32)],
        compiler_params=pltpu.CompilerParams(
            dimension_semantics=("arbitrary",), vmem_limit_bytes=VMEM_LIMIT),
        name="moe_dispatch",
    )(bstart, bend, bkt, rank, h2e)


def _expert_kernel(blk_lo_ref, blk_hi_ref, nused_ref, x_ref, wgu_lo_ref, wd_lo_ref, wgu_hi_ref,
                   wd_hi_ref, o_ref):
    i = pl.program_id(0)
    D = o_ref.shape[1]

    @pl.when(i < nused_ref[0])
    def _():
        xb = x_ref[:, :D].astype(BF16)
        w = x_ref[:, D:]

        def ffn(wgu_ref, wd_ref):
            gu = _dot(xb, wgu_ref[0])
            a = _silu(gu[:, :D_EXPERT]) * gu[:, D_EXPERT:]
            return _dot(a.astype(BF16), wd_ref[0])

        o_ref[...] = w[:, 0:1] * ffn(wgu_lo_ref, wd_lo_ref) + w[:, 1:2] * ffn(wgu_hi_ref, wd_hi_ref)

    @pl.when(i >= nused_ref[0])
    def _():
        o_ref[...] = jnp.zeros_like(o_ref)


def _experts(blk_lo, blk_hi, nused, xg, wgu, wd):
    cap, W = xg.shape
    D = W - LANES
    R = MOE_ROWS
    grid_spec = pltpu.PrefetchScalarGridSpec(
        num_scalar_prefetch=3,
        grid=(cap // R,),
        in_specs=[
            pl.BlockSpec((R, W), lambda i, lo, hi, nu: (jnp.minimum(i, nu[0] - 1), 0)),
            pl.BlockSpec((1, D, 2 * D_EXPERT), lambda i, lo, hi, nu: (lo[i], 0, 0)),
            pl.BlockSpec((1, D_EXPERT, D), lambda i, lo, hi, nu: (lo[i], 0, 0)),
            pl.BlockSpec((1, D, 2 * D_EXPERT), lambda i, lo, hi, nu: (hi[i], 0, 0)),
            pl.BlockSpec((1, D_EXPERT, D), lambda i, lo, hi, nu: (hi[i], 0, 0)),
        ],
        out_specs=pl.BlockSpec((R, D), lambda i, lo, hi, nu: (i, 0)),
    )
    return pl.pallas_call(
        _expert_kernel,
        grid_spec=grid_spec,
        out_shape=jax.ShapeDtypeStruct((cap, D), F32),
        compiler_params=pltpu.CompilerParams(
            dimension_semantics=("arbitrary",), vmem_limit_bytes=VMEM_LIMIT),
        name="moe_experts",
    )(blk_lo, blk_hi, nused, xg, wgu, wd, wgu, wd)


def _combine_kernel(dest_ref, og_ref, x_ref, mod_ref, fg_ref, o_ref, buf_sc, sem, *, final_norm):
    tb = x_ref.shape[1]

    def issue(t, c):
        _row_copy(og_ref, dest_ref[t], buf_sc, t, sem).start()
        return c
    lax.fori_loop(0, tb, issue, 0, unroll=8)

    def drain(t, c):
        _row_copy(og_ref, 0, buf_sc, 0, sem).wait()
        return c
    lax.fori_loop(0, tb, drain, 0, unroll=8)

    xo = x_ref[0] + mod_ref[0, 5:6, :] * buf_sc[...]
    if final_norm:
        xo = _rms(xo, fg_ref[...])
    o_ref[0] = xo


def _combine(dest, outg, x, mod, fg, final_norm):
    B, S, D = x.shape
    tb = TB_COMBINE
    ns = S // tb
    return pl.pallas_call(
        functools.partial(_combine_kernel, final_norm=final_norm),
        grid=(B, ns),
        in_specs=[
            pl.BlockSpec((tb,), lambda b, s: (b * ns + s,), memory_space=pltpu.SMEM),
            pl.BlockSpec(memory_space=pl.ANY),
            pl.BlockSpec((1, tb, D), lambda b, s: (b, s, 0)),
            pl.BlockSpec((1, N_MOD, D), lambda b, s: (b, 0, 0)),
            pl.BlockSpec((1, D), lambda b, s: (0, 0)),
        ],
        out_specs=pl.BlockSpec((1, tb, D), lambda b, s: (b, s, 0)),
        out_shape=jax.ShapeDtypeStruct((B, S, D), F32),
        scratch_shapes=[pltpu.VMEM((tb, D), F32), pltpu.SemaphoreType.DMA],
        compiler_params=pltpu.CompilerParams(
            dimension_semantics=("arbitrary", "arbitrary"), vmem_limit_bytes=VMEM_LIMIT),
        name="moe_combine",
    )(dest, outg, x, mod, fg)


def _rotate_half_cols(w):
    half = w.shape[-1] // 2
    return jnp.concatenate([-w[..., half:], w[..., :half]], axis=-1)


def _pack_layer_weights(l, w_in, w_uq, w_ukv, w_group, w_router, b_group, b_router,
                        w_expert_gate, w_expert_up, igate_b, fgate_b):
    D = w_in.shape[1]
    sizes = (M_QK_WIDTH, M_QK_WIDTH, M_V_WIDTH, M_V_WIDTH, M_HEADS, M_HEADS, Q_LORA, KV_LORA,
             QK_ROPE, 2 * D)
    pts = [sum(sizes[:i + 1]) for i in range(len(sizes) - 1)]
    wq, wk, wv, wo, wi, wf, wcq, wckv, wkr, wg = jnp.split(w_in[l], pts, axis=1)
    wmain = jnp.concatenate([wq, wk, wv, wo, wkr, _rotate_half_cols(wkr), wcq, wckv, wg],
                            axis=1).astype(BF16)
    wif = jnp.concatenate([wi, wf], axis=1)
    wifc = jnp.pad(wif, ((0, 0), (0, LANES - 2 * M_HEADS))).astype(BF16)
    wift = wif.T.astype(BF16)

    uq = w_uq[l].reshape(Q_LORA, A_HEADS, QK_NOPE + QK_ROPE)
    uq_rope = uq[:, :, QK_NOPE:]
    wuq = jnp.concatenate([uq[:, :, :QK_NOPE], uq_rope, _rotate_half_cols(uq_rope)], axis=-1)
    wuq = wuq.reshape(Q_LORA, A_HEADS * A_QK_PAD).astype(BF16)
    ukv = w_ukv[l].reshape(KV_LORA, A_HEADS, QK_NOPE + V_HEAD)
    wkn = ukv[:, :, :QK_NOPE].reshape(KV_LORA, A_HEADS * QK_NOPE).astype(BF16)
    wvv = ukv[:, :, QK_NOPE:].reshape(KV_LORA, A_HEADS * V_HEAD).astype(BF16)

    wr = jnp.concatenate([w_group[l], w_router[l]], axis=1)
    wr = jnp.pad(wr, ((0, 0), (0, LANES - wr.shape[1])))
    wrh = wr.astype(BF16)
    wrl = (wr - wrh.astype(F32)).astype(BF16)
    br = jnp.concatenate([b_group[l], b_router[l]])
    br = jnp.pad(br, (0, LANES - br.shape[0])).reshape(1, LANES)

    wgu = jnp.concatenate([w_expert_gate[l], w_expert_up[l]], axis=-1).astype(BF16)

    gb = jnp.concatenate([igate_b[l], fgate_b[l]])
    bias_c = jnp.pad(gb, (0, LANES - gb.shape[0])).reshape(1, LANES)
    bias_r = jnp.broadcast_to(gb[:, None], (2 * M_HEADS, LANES))
    return wmain, wifc, wift, wuq, wkn, wvv, wrh, wrl, br, wgu, bias_c, bias_r


def kernel(x, c, mod_w, mod_b, norm1_g, w_in, conv_w, conv_b, igate_b, fgate_b, q_norm_g,
           kv_norm_g, w_uq, w_ukv, w_branch_m, w_branch_a, w_out, norm2_g, w_group, b_group,
           w_router, b_router, w_expert_gate, w_expert_up, w_expert_down, final_norm_g):
    B, S, D = x.shape
    L = mod_w.shape[0]
    T = B * S
    assert S % TQ == 0 and TM_PROJ % MLSTM_CHUNK == 0 and S % TM_PROJ == 0 and S % TB_COMBINE == 0
    assert TQ == TK and TQ % CHUNK == 0 and T % DISPATCH_ROWS == 0
    assert A_HEADS % ATTN_HEADS_PER_STEP == 0 and V_HEAD == LANES
    assert N_BUCKETS <= LANES and TOP_K == 2

    pos = jnp.arange(S, dtype=F32)
    inv_freq = 1.0 / (ROPE_THETA ** (jnp.arange(0, QK_ROPE, 2, dtype=F32) / QK_ROPE))
    ang = pos[:, None] * inv_freq[None, :]
    cos = jnp.cos(ang)
    sin = jnp.sin(ang)
    z = jnp.zeros((S, QK_ROPE), F32)
    cs = jnp.concatenate([cos, cos, z, sin, sin, z], axis=1)

    mod_all = _modulation(c, mod_w, mod_b).reshape(L, B, N_MOD, D)

    cap = T + N_BUCKETS * MOE_ROWS
    n_blocks = cap // MOE_ROWS
    pairs = [(a, b) for a in range(EXPERTS_PER_GROUP) for b in range(a + 1, EXPERTS_PER_GROUP)]
    bucket_lo = jnp.array([g * EXPERTS_PER_GROUP + a for g in range(N_GROUPS) for a, _ in pairs], I32)
    bucket_hi = jnp.array([g * EXPERTS_PER_GROUP + b for g in range(N_GROUPS) for _, b in pairs], I32)

    for l in range(L):
        (wmain, wifc, wift, wuq, wkn, wvv, wrh, wrl, br, wgu, bias_c, bias_r) = _pack_layer_weights(
            l, w_in, w_uq, w_ukv, w_group, w_router, b_group, b_router, w_expert_gate,
            w_expert_up, igate_b, fgate_b)
        mod = mod_all[l]

        ym, o, gt, qa, ka, va = _in_proj(
            x, mod, norm1_g[l].reshape(1, D), wmain, wifc, wift, conv_w[l],
            conv_b[l].reshape(1, -1), q_norm_g[l].reshape(1, -1), kv_norm_g[l].reshape(1, -1),
            wuq, wkn, wvv, cs, bias_c, bias_r)
        ya = _attention(qa, ka, va)
        xn, h2e, ri, cnt = _merge(
            x, ym, o, ya, gt, mod, norm2_g[l].reshape(1, D), w_branch_m[l].astype(BF16),
            w_branch_a[l].astype(BF16), w_out[l].astype(BF16), wrh, wrl, br)

        counts = cnt[0, :N_BUCKETS].astype(I32)
        padded = (counts + MOE_ROWS - 1) // MOE_ROWS * MOE_ROWS
        bend = jnp.cumsum(padded).astype(I32)
        bstart = bend - padded
        blk_start = jnp.arange(n_blocks, dtype=I32) * MOE_ROWS
        blk_b = jnp.minimum(jnp.sum(bend[None, :] <= blk_start[:, None], axis=1), N_BUCKETS - 1)
        nused = (bend[N_BUCKETS - 1:] // MOE_ROWS).astype(I32)
        bkt = ri[:, 0, :].reshape(T)
        rank = ri[:, 1, :].reshape(T)

        xg, dest = _dispatch(bstart, bend, bkt, rank, h2e.reshape(T, D + LANES), cap)
        outg = _experts(bucket_lo[blk_b], bucket_hi[blk_b], nused, xg, wgu,
                        w_expert_down[l].astype(BF16))
        x = _combine(dest, outg, xn, mod, final_norm_g.reshape(1, D), final_norm=(l == L - 1))
    return x
```
